```python
import math
import jax
import jax.numpy as jnp
from jax import lax
import numpy as np

D_MODEL = 2048
BATCH = 8
SEQ = 2048
DEPTH = 1

GRID_W = 64
CTX_LEN = 256
NORM_EPS = 1e-6

RW_HEADS = 16
RW_HEAD_DIM = 64
RW_WIDTH = RW_HEADS * RW_HEAD_DIM
RW_DECAY_RANK = 96
RW_ICLR_RANK = 96
RW_GATE_RANK = 64
RW_GN_EPS = 64e-5
RW_COLS = 3 * RW_WIDTH + 2 * RW_DECAY_RANK + 2 * RW_ICLR_RANK + RW_GATE_RANK
RW_SPLITS = (RW_WIDTH, 2 * RW_WIDTH, 3 * RW_WIDTH, 3 * RW_WIDTH + 2 * RW_DECAY_RANK,
             3 * RW_WIDTH + 2 * RW_DECAY_RANK + 2 * RW_ICLR_RANK)

GDN_HEADS = 8
GDN_HEAD_DIM = 128
GDN_WIDTH = GDN_HEADS * GDN_HEAD_DIM
GDN_CONV = 5
GDN_CHUNK = 64
GDN_CONV_COLS = 3 * GDN_WIDTH
GDN_AB_COLS = 4 * GDN_HEADS

GATE_COLS = 2 * D_MODEL
IN_SPLITS = (RW_COLS, RW_COLS + GDN_CONV_COLS, RW_COLS + GDN_CONV_COLS + GDN_WIDTH,
             RW_COLS + GDN_CONV_COLS + GDN_WIDTH + GDN_AB_COLS)
N_IN = IN_SPLITS[-1] + GATE_COLS

FFN_HIDDEN = -(-8 * D_MODEL // (3 * 256)) * 256

kernel_name = 'hybrid_rwkv7_gdn_dit_block'


def rmsnorm(x, w):
    xf = x.astype(jnp.float32)
    y = xf * lax.rsqrt(jnp.mean(xf * xf, axis=-1, keepdims=True) + NORM_EPS)
    return (y * w.astype(jnp.float32)).astype(x.dtype)


def modulate(x, shift, scale):
    return x * (1.0 + scale) + shift


def l2norm(x):
    return x * lax.rsqrt(jnp.sum(x * x, axis=-1, keepdims=True) + NORM_EPS)


def grid_shift(p, rows):
    b, t, ch = p.shape
    g = p.reshape(b, rows, GRID_W, ch // 4, 4)
    left = jnp.pad(g[:, :, :-1, :, 0], ((0, 0), (0, 0), (1, 0), (0, 0)))
    right = jnp.pad(g[:, :, 1:, :, 1], ((0, 0), (0, 0), (0, 1), (0, 0)))
    up = jnp.pad(g[:, :-1, :, :, 2], ((0, 0), (1, 0), (0, 0), (0, 0)))
    down = jnp.pad(g[:, 1:, :, :, 3], ((0, 0), (0, 1), (0, 0), (0, 0)))
    return jnp.stack([left, right, up, down], axis=-1).reshape(b, t, ch)


def seq_shift(p):
    b, t, ch = p.shape
    g = p.reshape(b, t, ch // 2, 2)
    prev = jnp.pad(g[:, :-1, :, 0], ((0, 0), (1, 0), (0, 0)))
    nxt = jnp.pad(g[:, 1:, :, 1], ((0, 0), (0, 1), (0, 0)))
    return jnp.stack([prev, nxt], axis=-1).reshape(b, t, ch)


def depthwise_conv(x, w):
    return lax.conv_general_dilated(x, w[:, None, :], window_strides=(1,),
                                    padding=[(GDN_CONV // 2, GDN_CONV // 2)],
                                    dimension_numbers=('NWC', 'WIO', 'NWC'),
                                    feature_group_count=x.shape[-1])


def prefix_scan(scan_fn, ctx_in, lat_in, s0, reverse):
    flip = (lambda a: jnp.flip(a, axis=1)) if reverse else (lambda a: a)
    y_ctx, s_ctx = scan_fn(*[flip(a) for a in ctx_in], s0)
    y_lat, _ = scan_fn(*[flip(a) for a in lat_in], s_ctx)
    return flip(y_lat), flip(y_ctx)


def rwkv7_scan(r, w, k, v, kk, bb, s0):
    def step(s, inp):
        r_t, w_t, k_t, v_t, kk_t, b_t = inp
        sa = jnp.einsum('bhij,bhj->bhi', s, kk_t)
        s = s * w_t[:, :, None, :] - sa[..., None] * b_t[:, :, None, :] + v_t[..., None] * k_t[:, :, None, :]
        return s, jnp.einsum('bhij,bhj->bhi', s, r_t)
    xs = tuple(jnp.moveaxis(a, 1, 0) for a in (r, w, k, v, kk, bb))
    s_fin, y = lax.scan(step, s0, xs)
    return jnp.moveaxis(y, 0, 1), s_fin


def rwkv_features(p, p_shift, mu, w0, w2, a0, a2, g2, k_k, k_a):
    b, t, _ = p.shape
    xs = (p + (p_shift - p) * mu).astype(jnp.float32)
    r, k, v, wd, ad, gd = jnp.split(xs, RW_SPLITS, axis=-1)
    heads = lambda a: a.reshape(a.shape[:-1] + (RW_HEADS, RW_HEAD_DIM))
    lora_w = jnp.einsum('btdr,drc->btdc', jnp.tanh(wd.reshape(b, t, 2, RW_DECAY_RANK)), w2)
    w_log = -jax.nn.softplus(-(w0 + lora_w)) - 0.5
    decay = jnp.exp(-jnp.exp(w_log))
    iclr = jax.nn.sigmoid(a0 + jnp.einsum('btdr,drc->btdc', ad.reshape(b, t, 2, RW_ICLR_RANK), a2))
    gate = jax.nn.sigmoid(gd) @ g2
    kk = l2norm(heads(k * k_k))
    k_dir = heads(k[:, :, None, :] * (1.0 + (iclr - 1.0) * k_a))
    b_dir = kk[:, :, None] * heads(iclr)
    return heads(r), heads(v), gate, kk, heads(decay), k_dir, b_dir


def rwkv_dir_inputs(f, d):
    r, v, _, kk, decay, k_dir, b_dir = f
    return (r, decay[:, :, d], k_dir[:, :, d], v, kk, b_dir[:, :, d])


def rwkv_readout(y, f, r_k, ln_w, ln_b):
    r, v, gate, _, _, k_dir, _ = f
    b, t = y.shape[:2]
    mu = jnp.mean(y, axis=-1, keepdims=True)
    var = jnp.mean(jnp.square(y - mu), axis=-1, keepdims=True)
    yn = ((y - mu) * lax.rsqrt(var + RW_GN_EPS)).reshape(b, t, RW_WIDTH) * ln_w + ln_b
    k_bonus = jnp.mean(k_dir, axis=2)
    bonus = (jnp.sum(r * k_bonus * r_k, axis=-1, keepdims=True) * v).reshape(b, t, RW_WIDTH)
    return (yn + bonus) * gate


def gdn_chunked(q, k, v, g, beta, s0):
    b, t, h, dk = q.shape
    dv = v.shape[-1]
    n = t // GDN_CHUNK

    def chunks(a):
        a = a.reshape((b, n, GDN_CHUNK, h) + a.shape[3:])
        return jnp.moveaxis(a, (1, 3), (0, 2))

    qc, kc, vc = chunks(q * dk ** -0.5), chunks(k), chunks(v)
    bc = chunks(beta)
    gc = jnp.cumsum(chunks(g), axis=-1)
    idx = jnp.arange(GDN_CHUNK)
    incl = idx[:, None] >= idx[None, :]
    strict = idx[:, None] > idx[None, :]
    diff = gc[..., :, None] - gc[..., None, :]
    decay = jnp.where(incl, jnp.exp(jnp.where(incl, diff, 0.0)), 0.0)
    kb = kc * bc[..., None]
    lmat = jnp.where(strict, jnp.einsum('nbhik,nbhjk->nbhij', kb, kc) * decay, 0.0)
    rhs = jnp.concatenate([vc * bc[..., None], kb * jnp.exp(gc)[..., None]], axis=-1)
    sol = lax.linalg.triangular_solve(lmat, rhs, left_side=True, lower=True, unit_diagonal=True)
    u, wk = sol[..., :dv], sol[..., dv:]
    qk = jnp.where(incl, jnp.einsum('nbhik,nbhjk->nbhij', qc, kc) * decay, 0.0)
    q_dec = qc * jnp.exp(gc)[..., None]
    k_tail = kc * jnp.exp(gc[..., -1:] - gc)[..., None]
    g_last = jnp.exp(gc[..., -1])

    def step(s, inp):
        qk_i, u_i, w_i, qd_i, kt_i, gl_i = inp
        v_new = u_i - jnp.einsum('bhck,bhkv->bhcv', w_i, s)
        o = jnp.einsum('bhck,bhkv->bhcv', qd_i, s) + jnp.einsum('bhcj,bhjv->bhcv', qk_i, v_new)
        s = s * gl_i[..., None, None] + jnp.einsum('bhck,bhcv->bhkv', kt_i, v_new)
        return s, o

    s_fin, o = lax.scan(step, s0, (qk, u, wk, q_dec, k_tail, g_last))
    o = jnp.moveaxis(o, (0, 2), (1, 3)).reshape(b, t, h, dv)
    return o, s_fin


def gdn_features(p_conv, p_ab, conv_w, a_log, dt_bias):
    b, t, _ = p_conv.shape
    u = jax.nn.silu(depthwise_conv(p_conv, conv_w)).astype(jnp.float32)
    q, k, v = jnp.split(u, 3, axis=-1)
    heads = lambda a: a.reshape(b, t, GDN_HEADS, GDN_HEAD_DIM)
    ab = p_ab.astype(jnp.float32).reshape(b, t, 2, 2, GDN_HEADS)
    g = -jnp.exp(a_log.astype(jnp.float32)) * jax.nn.softplus(ab[:, :, 0] + dt_bias)
    beta = jax.nn.sigmoid(ab[:, :, 1])
    return l2norm(heads(q)), l2norm(heads(k)), heads(v), g, beta


def gdn_dir_inputs(f, d):
    q, k, v, g, beta = f
    return (q, k, v, g[:, :, d], beta[:, :, d])


def gdn_readout(o, z, norm_w):
    b, t = o.shape[:2]
    on = o * lax.rsqrt(jnp.mean(o * o, axis=-1, keepdims=True) + NORM_EPS) * norm_w
    return (on * jax.nn.silu(z.astype(jnp.float32)).reshape(o.shape)).reshape(b, t, GDN_WIDTH)


def gated_merge(o_a, o_b, gate_cols, p_a, p_b, w_out):
    g_a, g_b = jnp.split(gate_cols, 2, axis=-1)
    m = jax.nn.sigmoid(g_a) * (o_a @ p_a) + jax.nn.sigmoid(g_b) * (o_b @ p_b)
    return m @ w_out


def swiglu(h, w_gate_up, w_down):
    gate, up = jnp.split(h @ w_gate_up, 2, axis=-1)
    return (jax.nn.silu(gate) * up) @ w_down


def mix_tokens(h, hc, w_in, rw_mu, rw_w0, rw_w2, rw_a0, rw_a2, rw_g2, rw_k_k, rw_k_a, rw_r_k,
               rw_ln_w, rw_ln_b, gdn_conv_w, gdn_a_log, gdn_dt_bias, gdn_norm_w,
               merge_p_a, merge_p_b, w_out, need_ctx):
    rows = h.shape[1] // GRID_W
    bsz = h.shape[0]
    rw, cv, z, ab, gt = jnp.split(h @ w_in, IN_SPLITS, axis=-1)
    rw_c, cv_c, z_c, ab_c, gt_c = jnp.split(hc @ w_in, IN_SPLITS, axis=-1)

    fa = rwkv_features(rw, grid_shift(rw, rows), rw_mu, rw_w0, rw_w2, rw_a0, rw_a2, rw_g2, rw_k_k, rw_k_a)
    fa_c = rwkv_features(rw_c, seq_shift(rw_c), rw_mu, rw_w0, rw_w2, rw_a0, rw_a2, rw_g2, rw_k_k, rw_k_a)
    s0a = jnp.zeros((bsz, RW_HEADS, RW_HEAD_DIM, RW_HEAD_DIM), jnp.float32)
    out_a = [prefix_scan(rwkv7_scan, rwkv_dir_inputs(fa_c, d), rwkv_dir_inputs(fa, d), s0a, d == 1)
             for d in range(2)]
    o_a = rwkv_readout(out_a[0][0] + out_a[1][0], fa, rw_r_k, rw_ln_w, rw_ln_b).astype(h.dtype)

    fb = gdn_features(cv, ab, gdn_conv_w, gdn_a_log, gdn_dt_bias)
    fb_c = gdn_features(cv_c, ab_c, gdn_conv_w, gdn_a_log, gdn_dt_bias)
    s0b = jnp.zeros((bsz, GDN_HEADS, GDN_HEAD_DIM, GDN_HEAD_DIM), jnp.float32)
    out_b = [prefix_scan(gdn_chunked, gdn_dir_inputs(fb_c, d), gdn_dir_inputs(fb, d), s0b, d == 1)
             for d in range(2)]
    o_b = gdn_readout(out_b[0][0] + out_b[1][0], z, gdn_norm_w).astype(h.dtype)

    y = gated_merge(o_a, o_b, gt, merge_p_a, merge_p_b, w_out)
    if not need_ctx:
        return y, None
    o_a_c = rwkv_readout(out_a[0][1] + out_a[1][1], fa_c, rw_r_k, rw_ln_w, rw_ln_b).astype(hc.dtype)
    o_b_c = gdn_readout(out_b[0][1] + out_b[1][1], z_c, gdn_norm_w).astype(hc.dtype)
    return y, gated_merge(o_a_c, o_b_c, gt_c, merge_p_a, merge_p_b, w_out)


def setup_inputs(seed: int = 0) -> dict:
    key = jax.random.key(seed)
    ks = jax.random.split(key, 32)
    f32 = jnp.float32
    L, D = DEPTH, D_MODEL

    def nrm(k, shape, scale):
        return scale * jax.random.normal(k, shape, f32)

    dt = jnp.exp(jax.random.uniform(ks[21], (L, 2, GDN_HEADS), f32, math.log(1e-3), math.log(1e-1)))
    return {
        'x': nrm(ks[0], (BATCH, SEQ, D), 1.0),
        'c': nrm(ks[1], (BATCH, D), 1.0),
        'ctx': nrm(ks[2], (BATCH, CTX_LEN, D), 1.0),
        'c_ctx': nrm(ks[3], (D,), 1.0),
        'w_ada': nrm(ks[4], (L, D, 6 * D), 0.5 * D ** -0.5),
        'b_ada': nrm(ks[5], (L, 6 * D), 0.02),
        'norm1_w': 1.0 + nrm(ks[6], (L, D), 0.1),
        'w_in': nrm(ks[7], (L, D, N_IN), D ** -0.5),
        'rw_mu': jax.random.uniform(ks[8], (L, RW_COLS), f32),
        'rw_w0': jnp.linspace(-6.0, 2.0, RW_WIDTH, dtype=f32) + nrm(ks[9], (L, 2, RW_WIDTH), 0.1),
        'rw_w2': nrm(ks[10], (L, 2, RW_DECAY_RANK, RW_WIDTH), 0.5 * RW_DECAY_RANK ** -0.5),
        'rw_a0': nrm(ks[11], (L, 2, RW_WIDTH), 0.5),
        'rw_a2': nrm(ks[12], (L, 2, RW_ICLR_RANK, RW_WIDTH), RW_ICLR_RANK ** -0.5),
        'rw_g2': nrm(ks[13], (L, RW_GATE_RANK, RW_WIDTH), RW_GATE_RANK ** -0.5),
        'rw_k_k': 0.85 + nrm(ks[14], (L, RW_WIDTH), 0.1),
        'rw_k_a': 1.0 + nrm(ks[15], (L, RW_WIDTH), 0.1),
        'rw_r_k': nrm(ks[16], (L, RW_HEADS, RW_HEAD_DIM), 0.1),
        'rw_ln_w': 1.0 + nrm(ks[17], (L, RW_WIDTH), 0.1),
        'rw_ln_b': nrm(ks[18], (L, RW_WIDTH), 0.01),
        'gdn_conv_w': nrm(ks[19], (L, GDN_CONV, GDN_CONV_COLS), GDN_CONV ** -0.5),
        'gdn_a_log': jnp.log(jax.random.uniform(ks[20], (L, 2, GDN_HEADS), f32, 1.0, 16.0)),
        'gdn_dt_bias': dt + jnp.log(-jnp.expm1(-dt)),
        'gdn_norm_w': 1.0 + nrm(ks[22], (L, GDN_HEAD_DIM), 0.1),
        'merge_p_a': nrm(ks[23], (L, RW_WIDTH, D), RW_WIDTH ** -0.5),
        'merge_p_b': nrm(ks[24], (L, GDN_WIDTH, D), GDN_WIDTH ** -0.5),
        'w_out': nrm(ks[25], (L, D, D), D ** -0.5),
        'norm2_w': 1.0 + nrm(ks[26], (L, D), 0.1),
        'ffn_w_gate_up': nrm(ks[27], (L, D, 2 * FFN_HIDDEN), D ** -0.5),
        'ffn_w_down': nrm(ks[28], (L, FFN_HIDDEN, D), FFN_HIDDEN ** -0.5),
        'final_norm_w': 1.0 + nrm(ks[29], (D,), 0.1),
    }


def reference(x, c, ctx, c_ctx, w_ada, b_ada, norm1_w, w_in, rw_mu, rw_w0, rw_w2, rw_a0, rw_a2,
              rw_g2, rw_k_k, rw_k_a, rw_r_k, rw_ln_w, rw_ln_b, gdn_conv_w, gdn_a_log,
              gdn_dt_bias, gdn_norm_w, merge_p_a, merge_p_b, w_out, norm2_w, ffn_w_gate_up,
              ffn_w_down, final_norm_w):
    silu_c = jax.nn.silu(c)[:, None, :]
    silu_cc = jax.nn.silu(c_ctx)
    for layer in range(DEPTH):
        need_ctx = layer < DEPTH - 1
        mod = silu_c @ w_ada[layer] + b_ada[layer]
        mod_ctx = silu_cc @ w_ada[layer] + b_ada[layer]
        sh1, sc1, gt1, sh2, sc2, gt2 = jnp.split(mod, 6, axis=-1)
        csh1, csc1, cgt1, csh2, csc2, cgt2 = jnp.split(mod_ctx, 6, axis=-1)
        h = modulate(rmsnorm(x, norm1_w[layer]), sh1, sc1)
        hc = modulate(rmsnorm(ctx, norm1_w[layer]), csh1, csc1)
        y, yc = mix_tokens(h, hc, w_in[layer], rw_mu[layer], rw_w0[layer], rw_w2[layer],
                           rw_a0[layer], rw_a2[layer], rw_g2[layer], rw_k_k[layer], rw_k_a[layer],
                           rw_r_k[layer], rw_ln_w[layer], rw_ln_b[layer], gdn_conv_w[layer],
                           gdn_a_log[layer], gdn_dt_bias[layer], gdn_norm_w[layer],
                           merge_p_a[layer], merge_p_b[layer], w_out[layer], need_ctx)
        x = x + gt1 * y
        x = x + gt2 * swiglu(modulate(rmsnorm(x, norm2_w[layer]), sh2, sc2),
                             ffn_w_gate_up[layer], ffn_w_down[layer])
        if need_ctx:
            ctx = ctx + cgt1 * yc
            ctx = ctx + cgt2 * swiglu(modulate(rmsnorm(ctx, norm2_w[layer]), csh2, csc2),
                                      ffn_w_gate_up[layer], ffn_w_down[layer])
    return rmsnorm(x, final_norm_w)
```

```python
import functools
import math

import jax
import jax.numpy as jnp
from jax import lax
from jax.experimental import pallas as pl
from jax.experimental.pallas import tpu as pltpu

F32 = jnp.float32
BF16 = jnp.bfloat16

GRID_W = 64
NORM_EPS = 1e-6
RW_GN_EPS = 64e-5
CHUNK = 64
LANES = 128
HALF = 64
VMEM_LIMIT = 48 * 1024 * 1024


def _pick(n, pref, mult):
    t = min(pref, n)
    t -= t % mult
    while t > mult and n % t:
        t -= mult
    assert t >= mult and n % t == 0, (n, pref, mult)
    return t


def _cp(*sem):
    return pltpu.CompilerParams(dimension_semantics=sem, vmem_limit_bytes=VMEM_LIMIT)


def _sigmoid(x):
    return 1.0 / (1.0 + jnp.exp(-x))


def _silu(x):
    return x * _sigmoid(x)


def _softplus(x):
    return jnp.maximum(x, 0.0) + jnp.log(1.0 + jnp.exp(-jnp.abs(x)))


def _split(x):
    hi = x.astype(BF16)
    lo = (x - hi.astype(F32)).astype(BF16)
    return hi, lo


_NN = (((1,), (0,)), ((), ()))
_NT = (((1,), (1,)), ((), ()))
_TN = (((0,), (0,)), ((), ()))


def _dg(a, b, dn):
    return lax.dot_general(a, b, dn, preferred_element_type=F32)


def _mm(a, b, dn=_NN, passes=1):
    if passes == 1:
        return _dg(a.astype(BF16), b.astype(BF16), dn)
    ah, al = _split(a)
    bh, bl = _split(b)
    return _dg(ah, bh, dn) + (_dg(ah, bl, dn) + _dg(al, bh, dn))


def _mm_exact_lhs(a_bf, b, dn=_NN):
    bh, bl = _split(b)
    return _dg(a_bf, bh, dn) + _dg(a_bf, bl, dn)


def _iota(shape, dim):
    return lax.broadcasted_iota(jnp.int32, shape, dim)


def _ada_kernel(c_ref, w_ref, b_ref, o_ref):
    s = _silu(c_ref[...])
    o_ref[...] = _dg(s.astype(BF16), w_ref[...].astype(BF16), _NN) + b_ref[...]


def _ada(cc, w, b):
    m, d = cc.shape
    n = w.shape[1]
    tn = _pick(n, 1024, LANES)
    return pl.pallas_call(
        _ada_kernel, grid=(n // tn,),
        in_specs=[pl.BlockSpec((m, d), lambda j: (0, 0)),
                  pl.BlockSpec((d, tn), lambda j: (0, j)),
                  pl.BlockSpec((1, tn), lambda j: (0, j))],
        out_specs=pl.BlockSpec((m, tn), lambda j: (0, j)),
        out_shape=jax.ShapeDtypeStruct((m, n), F32),
        compiler_params=_cp("parallel"), name="ada")(cc, w, b)


def _normmod_kernel(x_ref, nw_ref, sh_ref, sc_ref, o_ref):
    x = x_ref[...]
    y = x * lax.rsqrt(jnp.mean(x * x, axis=-1, keepdims=True) + NORM_EPS) * nw_ref[...]
    o_ref[...] = (y * (1.0 + sc_ref[...]) + sh_ref[...]).astype(o_ref.dtype)


def _normmod(x2, nw, sh, sc, rows_per_mod):
    m, d = x2.shape
    tm = _pick(rows_per_mod, 512, 8)
    per = rows_per_mod // tm
    return pl.pallas_call(
        _normmod_kernel, grid=(m // tm,),
        in_specs=[pl.BlockSpec((tm, d), lambda i: (i, 0)),
                  pl.BlockSpec((1, d), lambda i: (0, 0)),
                  pl.BlockSpec((None, 1, d), lambda i: (i // per, 0, 0)),
                  pl.BlockSpec((None, 1, d), lambda i: (i // per, 0, 0))],
        out_specs=pl.BlockSpec((tm, d), lambda i: (i, 0)),
        out_shape=jax.ShapeDtypeStruct((m, d), BF16),
        compiler_params=_cp("parallel"), name="normmod")(x2, nw, sh, sc)


def _mm_kernel(a_ref, w_ref, o_ref):
    o_ref[...] = _dg(a_ref[...], w_ref[...], _NN).astype(o_ref.dtype)


def _matmul(a, w, out_dtype=F32, tm=1024, tn=512, name="matmul"):
    m, k = a.shape
    n = w.shape[1]
    tm = _pick(m, tm, 8)
    tn = _pick(n, tn, LANES)
    return pl.pallas_call(
        _mm_kernel, grid=(m // tm, n // tn),
        in_specs=[pl.BlockSpec((tm, k), lambda i, j: (i, 0)),
                  pl.BlockSpec((k, tn), lambda i, j: (0, j))],
        out_specs=pl.BlockSpec((tm, tn), lambda i, j: (i, j)),
        out_shape=jax.ShapeDtypeStruct((m, n), out_dtype),
        compiler_params=_cp("parallel", "parallel"), name=name)(a, w)


def _shiftmix_kernel(mode, t_len, tt, p_ref, prev_ref, next_ref, mu_ref, o_ref):
    i = pl.program_id(1)
    p = p_ref[...]
    shp = p.shape
    t = i * tt + _iota(shp, 0)
    lane = _iota(shp, 1)
    if mode == "grid":
        cl = lane & 3
        w = t & (GRID_W - 1)
        left = jnp.where(w != 0, pltpu.roll(p, 1, 0), 0.0)
        right = jnp.where(w != GRID_W - 1, pltpu.roll(p, tt - 1, 0), 0.0)
        if tt > GRID_W:
            up = jnp.concatenate([prev_ref[...], p[:tt - GRID_W]], axis=0)
            down = jnp.concatenate([p[GRID_W:], next_ref[...]], axis=0)
        else:
            up, down = prev_ref[...], next_ref[...]
        up = jnp.where(t >= GRID_W, up, 0.0)
        down = jnp.where(t < t_len - GRID_W, down, 0.0)
        sh = jnp.where(cl == 0, left, jnp.where(cl == 1, right, jnp.where(cl == 2, up, down)))
    else:
        prv = jnp.where(t != 0, pltpu.roll(p, 1, 0), 0.0)
        nxt = jnp.where(t != t_len - 1, pltpu.roll(p, tt - 1, 0), 0.0)
        sh = jnp.where((lane & 1) == 0, prv, nxt)
    o_ref[...] = p + (sh - p) * mu_ref[...]


def _shiftmix(p3, mu, mode):
    b, t_len, cp = p3.shape
    cb = _pick(cp, 512, LANES)
    tt = _pick(t_len, 512, GRID_W) if mode == "grid" else t_len
    hb = tt // GRID_W if mode == "grid" else 1
    nh = t_len // GRID_W
    halo = GRID_W if mode == "grid" else 8
    kern = functools.partial(_shiftmix_kernel, mode, t_len, tt)
    return pl.pallas_call(
        kern, grid=(b, t_len // tt, cp // cb),
        in_specs=[pl.BlockSpec((None, tt, cb), lambda bi, i, c: (bi, i, c)),
                  pl.BlockSpec((None, halo, cb), lambda bi, i, c: (bi, jnp.maximum(i * hb - 1, 0), c)),
                  pl.BlockSpec((None, halo, cb), lambda bi, i, c: (bi, jnp.minimum((i + 1) * hb, nh - 1), c)),
                  pl.BlockSpec((1, cb), lambda bi, i, c: (0, c))],
        out_specs=pl.BlockSpec((None, tt, cb), lambda bi, i, c: (bi, i, c)),
        out_shape=jax.ShapeDtypeStruct(p3.shape, F32),
        compiler_params=_cp("parallel", "parallel", "parallel"), name="shiftmix")(p3, p3, p3, mu)


def _segsum(x, e):
    outs = []
    for g in range(x.shape[1] // LANES):
        outs.append(_mm_exact_lhs_rhs(x[:, g * LANES:(g + 1) * LANES], e))
    return outs[0] if len(outs) == 1 else jnp.concatenate(outs, axis=1)


def _mm_exact_lhs_rhs(x, e_bf):
    hi, lo = _split(x)
    return _dg(hi, e_bf, _NN) + _dg(lo, e_bf, _NN)


def _rwfeat_kernel(wr, wdp, adp, latent, xs_ref, w0_ref, w2_ref, a0_ref, a2_ref, g2_ref, kk_ref_, ka_ref,
                   rk_ref, e_ref, *outs):
    if latent:
        r_o, v_o, kk_o, lw_o, kd_o, bd_o, gate_o, bonus_o = outs
    else:
        r_o, v_o, kk_o, lw_o, kd_o, bd_o = outs
    e = e_ref[...]
    r = xs_ref[:, 0:wr]
    k = xs_ref[:, wr:2 * wr]
    v = xs_ref[:, 2 * wr:3 * wr]
    wd = xs_ref[:, 3 * wr:3 * wr + wdp]
    ad = xs_ref[:, 3 * wr + wdp:3 * wr + wdp + adp]
    lw = _dg(jnp.tanh(wd).astype(BF16), w2_ref[...], _NN) + w0_ref[...]
    w_log = -_softplus(-lw) - 0.5
    logw = -jnp.exp(w_log)
    iclr = _sigmoid(_dg(ad.astype(BF16), a2_ref[...], _NN) + a0_ref[...])
    kk0 = k * kk_ref_[...]
    kk = kk0 * lax.rsqrt(_segsum(kk0 * kk0, e) + NORM_EPS)
    r_o[...] = r
    v_o[...] = v
    kk_o[...] = kk
    ka = ka_ref[...]
    kdirs = []
    for d in range(2):
        ic = iclr[:, d * wr:(d + 1) * wr]
        kd = k * (1.0 + (ic - 1.0) * ka)
        kdirs.append(kd)
        lw_o[d] = logw[:, d * wr:(d + 1) * wr]
        kd_o[d] = kd
        bd_o[d] = kk * ic
    if latent:
        gd = xs_ref[:, 3 * wr + wdp + adp:]
        gate_o[...] = _dg(_sigmoid(gd).astype(BF16), g2_ref[...], _NN)
        kb = 0.5 * (kdirs[0] + kdirs[1])
        bonus_o[...] = _segsum(r * kb * rk_ref[...], e) * v


def _rwfeat(xs, prm, latent):
    m, cp = xs.shape
    wr, wdp, adp = prm["wr"], prm["wdp"], prm["adp"]
    tm = _pick(m, 128, 8)
    row = lambda i: (i, 0)
    full = lambda i: (0, 0)
    w_specs = [pl.BlockSpec(prm[n].shape, full) for n in
               ("w0", "w2", "a0", "a2", "g2", "k_k", "k_a", "r_k", "e64")]
    tok = pl.BlockSpec((tm, wr), row)
    tok2 = pl.BlockSpec((2, tm, wr), lambda i: (0, i, 0))
    s1 = jax.ShapeDtypeStruct((m, wr), F32)
    s2 = jax.ShapeDtypeStruct((2, m, wr), F32)
    out_specs = [tok, tok, tok, tok2, tok2, tok2]
    out_shape = [s1, s1, s1, s2, s2, s2]
    if latent:
        out_specs += [tok, tok]
        out_shape += [s1, s1]
    kern = functools.partial(_rwfeat_kernel, wr, wdp, adp, latent)
    return pl.pallas_call(
        kern, grid=(m // tm,),
        in_specs=[pl.BlockSpec((tm, cp), row)] + w_specs,
        out_specs=out_specs, out_shape=out_shape,
        compiler_params=_cp("parallel"), name="rwfeat")(
            xs, *[prm[n] for n in ("w0", "w2", "a0", "a2", "g2", "k_k", "k_a", "r_k", "e64")])


INV_PASSES = 3
PREP_PASSES = 1


def _stack(x, h0):
    return jnp.concatenate([jnp.where(h0, x, 0.0), jnp.where(h0, 0.0, x)], axis=0)


def _tri_inverse(a, eye2, h0):
    p = eye2 - a
    x = a
    for _ in range(int(math.log2(CHUNK)) - 1):
        x = _mm(x, _stack(x, h0), passes=INV_PASSES)
        p = p + _mm(p, _stack(x, h0), passes=INV_PASSES)
    return p


def _rwprep_chunk(rev, need_out, r, v, kk, lw, k, bb):
    c = CHUNK
    shp = (c, LANES)
    row = _iota(shp, 0)
    lane = _iota(shp, 1)
    col = lane & (HALF - 1)
    h0 = lane < HALF
    if rev:
        strict, incl = col > row, col >= row
        tri = (_iota((c, c), 1) >= _iota((c, c), 0))
        last, mid = 0, c // 2
    else:
        strict, incl = col < row, col <= row
        tri = (_iota((c, c), 1) <= _iota((c, c), 0))
        last, mid = c - 1, c // 2 - 1
    tri_bf = jnp.where(tri, 1.0, 0.0).astype(BF16)
    g = _mm_exact_lhs(tri_bf, lw)
    gm = g - lw
    gtot = g[last:last + 1, :]
    rho = g[mid:mid + 1, :]
    e_inv = jnp.exp(rho - g)
    kkd = kk * jnp.exp(gm - rho)
    bbi = bb * e_inv
    ki = k * e_inv
    et = jnp.exp(gtot - g)
    kt = k * et
    bbt = bb * et
    pc = jnp.exp(gtot)
    kkd0 = kk * jnp.exp(gm)
    eye2 = jnp.where(col == row, 1.0, 0.0)
    if need_out:
        rd = r * jnp.exp(g - rho)
        lhs = jnp.concatenate([kkd, rd], axis=0)
    else:
        lhs = kkd
    rhs = jnp.concatenate([_stack(bbi, h0), _stack(ki, h0)], axis=0)
    ab = _mm(lhs, rhs, _NT, PREP_PASSES)
    a = jnp.where(strict, ab[:c, :LANES], 0.0)
    bm = jnp.where(strict, ab[:c, LANES:], 0.0)
    t = _tri_inverse(a, eye2, h0)
    sv = _stack(v, h0)
    bv = _mm(bm, sv, passes=PREP_PASSES)
    wu = _mm(t, jnp.concatenate([_stack(kkd0, h0), _stack(bv, h0)], axis=1), passes=PREP_PASSES)
    w = wu[:, :LANES]
    u0 = wu[:, LANES:]
    bdm = (_iota((LANES, LANES), 0) < HALF) == (_iota((LANES, LANES), 1) < HALF)
    mn = jnp.where(bdm, _mm(bbt, w, _TN, PREP_PASSES), 0.0)
    nnt = jnp.where(bdm, _mm(v, kt, _TN, PREP_PASSES) - _mm(u0, bbt, _TN, PREP_PASSES), 0.0)
    if not need_out:
        return None, None, mn, nnt, pc
    ar = jnp.where(incl, ab[c:, :LANES], 0.0)
    br = jnp.where(incl, ab[c:, LANES:], 0.0)
    arwu = _mm(ar, jnp.concatenate([_stack(w, h0), _stack(u0, h0)], axis=1), passes=PREP_PASSES)
    qp = r * jnp.exp(g) - arwu[:, :LANES]
    o0 = _mm(br, sv, passes=PREP_PASSES) - arwu[:, LANES:]
    return qp, o0, mn, nnt, pc


def _rwprep_kernel(tc, need_out, r_ref, v_ref, kk_ref, lwf_ref, lwr_ref, kf_ref, kr_ref, bf_ref, br_ref, *outs):
    per = 5 if need_out else 3
    outs_d = (outs[:per], outs[per:])
    ins_d = ((lwf_ref, kf_ref, bf_ref), (lwr_ref, kr_ref, br_ref))

    def body(ci, carry):
        sl = pl.ds(pl.multiple_of(ci * CHUNK, CHUNK), CHUNK)
        r, v, kk = r_ref[sl, :], v_ref[sl, :], kk_ref[sl, :]
        for d in range(2):
            lw_ref, k_ref, b_ref = ins_d[d]
            qp, o0, mn, nnt, pc = _rwprep_chunk(d == 1, need_out, r, v, kk, lw_ref[sl, :], k_ref[sl, :], b_ref[sl, :])
            o = outs_d[d]
            if need_out:
                o[0][sl, :] = qp
                o[1][sl, :] = o0
                o = o[2:]
            o[0][ci] = mn
            o[1][ci] = nnt
            o[2][pl.ds(ci, 1), :] = pc
        return carry

    lax.fori_loop(0, tc, body, 0)


def _prep_out(nseq, t_len, need_out):
    nc = t_len // CHUNK
    shapes = []
    if need_out:
        shapes += [jax.ShapeDtypeStruct((nseq, t_len, LANES), F32)] * 2
    shapes += [jax.ShapeDtypeStruct((nseq, nc, LANES, LANES), F32)] * 2
    shapes += [jax.ShapeDtypeStruct((nseq, nc, LANES), F32)]
    return shapes


def _prep_out_specs(tc, need_out, seq_of):
    specs = []
    if need_out:
        specs += [pl.BlockSpec((None, tc * CHUNK, LANES), lambda b, p, j: (seq_of(b, p), j, 0))] * 2
    specs += [pl.BlockSpec((None, tc, LANES, LANES), lambda b, p, j: (seq_of(b, p), j, 0, 0))] * 2
    specs += [pl.BlockSpec((None, tc, LANES), lambda b, p, j: (seq_of(b, p), j, 0))]
    return specs


def _chunks_per_step(nc):
    return 8 if nc % 8 == 0 else nc


def _rwprep(feat, bsz, t_len, need_out):
    r, v, kk, lw, kd, bd = feat[:6]
    wr = r.shape[1]
    npair = wr // LANES
    nc = t_len // CHUNK
    tc = _chunks_per_step(nc)
    nblk = nc // tc
    tok = pl.BlockSpec((tc * CHUNK, LANES), lambda b, p, j: (b * nblk + j, p))
    tok_d = lambda d: pl.BlockSpec((None, tc * CHUNK, LANES), lambda b, p, j: (d, b * nblk + j, p))
    seq_of = lambda b, p: b * npair + p
    kern = functools.partial(_rwprep_kernel, tc, need_out)
    return pl.pallas_call(
        kern, grid=(bsz, npair, nblk),
        in_specs=[tok, tok, tok, tok_d(0), tok_d(1), tok_d(0), tok_d(1), tok_d(0), tok_d(1)],
        out_specs=_prep_out_specs(tc, need_out, seq_of) * 2,
        out_shape=_prep_out(bsz * npair, t_len, need_out) * 2,
        compiler_params=_cp("parallel", "parallel", "parallel"), name="rwprep")(
            r, v, kk, lw, lw, kd, kd, bd, bd)


SCAN_PASSES = 1


def _scan_kernel(tc, need_out, *refs):
    per = 5 if need_out else 3
    ins = (refs[:per], refs[per:2 * per])
    s0_ref = refs[2 * per]
    rest = refs[2 * per + 1:]
    if need_out:
        o_refs = rest[:2]
        rest = rest[2:]
    sfin_ref, s_scr = rest
    j = pl.program_id(1)

    @pl.when(j == 0)
    def _():
        s_scr[...] = s0_ref[...]

    def body(i, carry):
        new = []
        for d in range(2):
            s = carry[d]
            ci = i if d == 0 else tc - 1 - i
            refs_d = ins[d]
            if need_out:
                sl = pl.ds(pl.multiple_of(ci * CHUNK, CHUNK), CHUNK)
                o_refs[d][sl, :] = _mm(refs_d[0][sl, :], s, _NT, SCAN_PASSES) + refs_d[1][sl, :]
                refs_d = refs_d[2:]
            mn = refs_d[0][ci]
            nnt = refs_d[1][ci]
            pc = refs_d[2][pl.ds(ci, 1), :]
            new.append(s * pc - _mm(s, mn, _NT, SCAN_PASSES) + nnt)
        return tuple(new)

    s_f, s_r = lax.fori_loop(0, tc, body, (s_scr[0], s_scr[1]))
    s_scr[0] = s_f
    s_scr[1] = s_r

    @pl.when(j == pl.num_programs(1) - 1)
    def _():
        sfin_ref[...] = s_scr[...]


def _scan(prep_f, prep_r, s0, need_out):
    mn = prep_f[-3]
    nseq, nc = mn.shape[0], mn.shape[1]
    tc = _chunks_per_step(nc)
    nblk = nc // tc

    def specs(rev):
        jj = (lambda j: nblk - 1 - j) if rev else (lambda j: j)
        sp = []
        if need_out:
            sp += [pl.BlockSpec((None, tc * CHUNK, LANES), lambda s, j: (s, jj(j), 0))] * 2
        sp += [pl.BlockSpec((None, tc, LANES, LANES), lambda s, j: (s, jj(j), 0, 0))] * 2
        sp += [pl.BlockSpec((None, tc, LANES), lambda s, j: (s, jj(j), 0))]
        return sp

    st_spec = pl.BlockSpec((2, None, LANES, LANES), lambda s, j: (0, s, 0, 0))
    out_specs, out_shape = [], []
    if need_out:
        out_specs += [specs(False)[0], specs(True)[0]]
        out_shape += [jax.ShapeDtypeStruct((nseq, nc * CHUNK, LANES), F32)] * 2
    out_specs.append(st_spec)
    out_shape.append(jax.ShapeDtypeStruct((2, nseq, LANES, LANES), F32))
    kern = functools.partial(_scan_kernel, tc, need_out)
    return pl.pallas_call(
        kern, grid=(nseq, nblk),
        in_specs=specs(False) + specs(True) + [st_spec],
        out_specs=out_specs, out_shape=out_shape,
        scratch_shapes=[pltpu.VMEM((2, LANES, LANES), F32)],
        compiler_params=_cp("parallel", "arbitrary"), name="scan")(*prep_f, *prep_r, s0)


def _rw_readout_kernel(of_ref, or_ref, bonus_ref, gate_ref, lnw_ref, lnb_ref, e_ref, o_ref):
    e = e_ref[...]
    y = of_ref[...] + or_ref[...]
    inv = 1.0 / HALF
    mu = _segsum(y, e) * inv
    yc = y - mu
    var = _segsum(yc * yc, e) * inv
    yn = yc * lax.rsqrt(var + RW_GN_EPS) * lnw_ref[...] + lnb_ref[...]
    o_ref[...] = ((yn + bonus_ref[...]) * gate_ref[...]).astype(o_ref.dtype)


def _rw_readout(o_f, o_r, bonus, gate, lnw, lnb, e64, bsz):
    nseq, t_len, _ = o_f.shape
    npair = nseq // bsz
    tt = _pick(t_len, 512, 8)
    nt = t_len // tt
    seq = pl.BlockSpec((None, tt, LANES), lambda b, p, i: (b * npair + p, i, 0))
    tok = pl.BlockSpec((tt, LANES), lambda b, p, i: (b * nt + i, p))
    par = pl.BlockSpec((1, LANES), lambda b, p, i: (0, p))
    return pl.pallas_call(
        _rw_readout_kernel, grid=(bsz, npair, nt),
        in_specs=[seq, seq, tok, tok, par, par, pl.BlockSpec((LANES, LANES), lambda b, p, i: (0, 0))],
        out_specs=tok, out_shape=jax.ShapeDtypeStruct(bonus.shape, BF16),
        compiler_params=_cp("parallel", "parallel", "parallel"), name="rw_readout")(
            o_f, o_r, bonus, gate, lnw, lnb, e64)


def _gconv_kernel(taps, nq, scale, p_ref, prev_ref, next_ref, w_ref, o_ref):
    i = pl.program_id(1)
    cb = pl.program_id(2)
    main = p_ref[...]
    tt = main.shape[0]
    prev = jnp.where(i > 0, prev_ref[...], 0.0)
    nxt = jnp.where(i < pl.num_programs(1) - 1, next_ref[...], 0.0)
    xcat = jnp.concatenate([prev, main, nxt], axis=0)
    n = tt + 16
    half = taps // 2
    acc = None
    for j in range(taps):
        sh = (half - j) % n
        xs = xcat if sh == 0 else pltpu.roll(xcat, sh, 0)
        term = xs[8:8 + tt] * w_ref[j:j + 1, :]
        acc = term if acc is None else acc + term
    u = _silu(acc)
    un = u * lax.rsqrt(jnp.sum(u * u, axis=-1, keepdims=True) + NORM_EPS)
    un = un * jnp.where(cb < nq, scale, 1.0)
    o_ref[...] = jnp.where(cb < 2 * nq, un, u)


def _gconv(p3, conv_w, nq, scale):
    b, t_len, c3 = p3.shape
    taps = conv_w.shape[0]
    tt = _pick(t_len, 512, 8)
    hb = tt // 8
    nh = t_len // 8
    kern = functools.partial(_gconv_kernel, taps, nq, scale)
    return pl.pallas_call(
        kern, grid=(b, t_len // tt, c3 // LANES),
        in_specs=[pl.BlockSpec((None, tt, LANES), lambda bi, i, c: (bi, i, c)),
                  pl.BlockSpec((None, 8, LANES), lambda bi, i, c: (bi, jnp.maximum(i * hb - 1, 0), c)),
                  pl.BlockSpec((None, 8, LANES), lambda bi, i, c: (bi, jnp.minimum((i + 1) * hb, nh - 1), c)),
                  pl.BlockSpec((taps, LANES), lambda bi, i, c: (0, c))],
        out_specs=pl.BlockSpec((None, tt, LANES), lambda bi, i, c: (bi, i, c)),
        out_shape=jax.ShapeDtypeStruct(p3.shape, F32),
        compiler_params=_cp("parallel", "parallel", "parallel"), name="gconv")(p3, p3, p3, conv_w)


def _gfeat_kernel(hg, ab_ref, nega_ref, dtb_ref, o_ref):
    ab = ab_ref[...]
    tm = ab.shape[0]
    lane = _iota(ab.shape, 1)
    g = jnp.where(lane < 2 * hg, nega_ref[...] * _softplus(ab + dtb_ref[...]), 0.0)
    beta = _sigmoid(ab)
    rr = _iota((tm, tm), 0)
    cc = _iota((tm, tm), 1)
    same = (rr // CHUNK) == (cc // CHUNK)
    lf = jnp.where(same & (cc <= rr), 1.0, 0.0).astype(BF16)
    lr = jnp.where(same & (cc >= rr), 1.0, 0.0).astype(BF16)
    gcf = _mm_exact_lhs(lf, g)
    gcr = _mm_exact_lhs(lr, g)
    gc = jnp.where(lane < hg, gcf, gcr)
    o_ref[...] = jnp.where(lane < 2 * hg, gc, beta)


def _gfeat(ab, nega, dtb, hg):
    m = ab.shape[0]
    tm = _pick(m, 256, CHUNK)
    return pl.pallas_call(
        functools.partial(_gfeat_kernel, hg), grid=(m // tm,),
        in_specs=[pl.BlockSpec((tm, LANES), lambda i: (i, 0)),
                  pl.BlockSpec((1, LANES), lambda i: (0, 0)),
                  pl.BlockSpec((1, LANES), lambda i: (0, 0))],
        out_specs=pl.BlockSpec((tm, LANES), lambda i: (i, 0)),
        out_shape=jax.ShapeDtypeStruct((m, LANES), F32),
        compiler_params=_cp("parallel"), name="gfeat")(ab, nega, dtb)


def _col_from_row(rowv, eye):
    c = rowv.shape[1]
    return jnp.sum(jnp.where(eye, jnp.broadcast_to(rowv, (c, c)), 0.0), axis=1, keepdims=True)


def _gprep_chunk(need_out, q, k, v, rows):
    c = CHUNK
    shp = (c, LANES)
    row = _iota(shp, 0)
    lane = _iota(shp, 1)
    col = lane & (HALF - 1)
    f0 = lane < HALF
    eye = _iota((c, c), 0) == _iota((c, c), 1)
    ahead = jnp.where(f0, row - col, col - row)
    strict = ahead > 0
    incl = ahead >= 0
    gr = (rows[0:1, :], rows[1:2, :])
    gcol = [_col_from_row(x, eye) for x in gr]
    bcol = [_col_from_row(rows[2 + d:3 + d, :], eye) for d in range(2)]
    grow2 = jnp.concatenate(gr, axis=1)
    gcol2 = jnp.where(f0, gcol[0], gcol[1])
    bcol2 = jnp.where(f0, bcol[0], bcol[1])
    decay = jnp.where(incl, jnp.exp(jnp.where(incl, gcol2 - grow2, 0.0)), 0.0)
    kk2 = jnp.concatenate([k, k], axis=0)
    if need_out:
        kq = _mm(jnp.concatenate([k, q], axis=0), kk2, _NT, PREP_PASSES)
    else:
        kq = _mm(k, kk2, _NT, PREP_PASSES)
    lmat = jnp.where(strict, kq[:c] * bcol2 * decay, 0.0)
    eye2 = jnp.where(col == row, 1.0, 0.0)
    t = _tri_inverse(lmat, eye2, f0)
    glast = (gr[0][:, c - 1:c], gr[1][:, 0:1])
    egc = [jnp.exp(gcol[d]) for d in range(2)]
    rhs = [jnp.concatenate([v * bcol[d], k * (bcol[d] * egc[d])], axis=1) for d in range(2)]
    rhs2 = jnp.concatenate(rhs, axis=0)
    sol = [_mm(jnp.where(f0, t, 0.0), rhs2, passes=PREP_PASSES),
           _mm(jnp.where(f0, 0.0, t), rhs2, passes=PREP_PASSES)]
    out = []
    if need_out:
        qk = jnp.where(incl, kq[c:] * decay, 0.0)
        sol2 = jnp.concatenate(sol, axis=0)
        xx = [_mm(jnp.where(f0, qk, 0.0), sol2, passes=PREP_PASSES),
              _mm(jnp.where(f0, 0.0, qk), sol2, passes=PREP_PASSES)]
    for d in range(2):
        u = sol[d][:, :LANES]
        wk = sol[d][:, LANES:]
        ktail = k * jnp.exp(glast[d] - gcol[d])
        mn = _mm(ktail, wk, _TN, PREP_PASSES)
        nnt = _mm(u, ktail, _TN, PREP_PASSES)
        pc = jnp.broadcast_to(jnp.exp(glast[d]), (1, LANES))
        if need_out:
            qp = q * egc[d] - xx[d][:, LANES:]
            o0 = xx[d][:, :LANES]
        else:
            qp = o0 = None
        out.append((qp, o0, mn, nnt, pc))
    return out


def _gprep_kernel(tc, need_out, q_ref, k_ref, v_ref, rows_ref, *outs):
    per = 5 if need_out else 3
    outs_d = (outs[:per], outs[per:])

    def body(ci, carry):
        sl = pl.ds(pl.multiple_of(ci * CHUNK, CHUNK), CHUNK)
        q = q_ref[sl, :] if need_out else None
        res = _gprep_chunk(need_out, q, k_ref[sl, :], v_ref[sl, :], rows_ref[ci])
        for d in range(2):
            qp, o0, mn, nnt, pc = res[d]
            o = outs_d[d]
            if need_out:
                o[0][sl, :] = qp
                o[1][sl, :] = o0
                o = o[2:]
            o[0][ci] = mn
            o[1][ci] = nnt
            o[2][pl.ds(ci, 1), :] = pc
        return carry

    lax.fori_loop(0, tc, body, 0)


def _gprep(qkv, rows, bsz, t_len, hg, need_out):
    nc = t_len // CHUNK
    tc = _chunks_per_step(nc)
    nblk = nc // tc
    tok = lambda off: pl.BlockSpec((tc * CHUNK, LANES), lambda b, h, j: (b * nblk + j, off + h))
    seq_of = lambda b, h: b * hg + h
    kern = functools.partial(_gprep_kernel, tc, need_out)
    return pl.pallas_call(
        kern, grid=(bsz, hg, nblk),
        in_specs=[tok(0), tok(hg), tok(2 * hg),
                  pl.BlockSpec((None, None, tc, 4, CHUNK), lambda b, h, j: (b, h, j, 0, 0))],
        out_specs=_prep_out_specs(tc, need_out, seq_of) * 2,
        out_shape=_prep_out(bsz * hg, t_len, need_out) * 2,
        compiler_params=_cp("parallel", "parallel", "parallel"), name="gprep")(qkv, qkv, qkv, rows)


def _g_readout_kernel(of_ref, or_ref, z_ref, nw_ref, o_ref):
    o = of_ref[...] + or_ref[...]
    on = o * lax.rsqrt(jnp.mean(o * o, axis=-1, keepdims=True) + NORM_EPS) * nw_ref[...]
    o_ref[...] = (on * _silu(z_ref[...])).astype(o_ref.dtype)


def _g_readout(o_f, o_r, z, nw, bsz):
    nseq, t_len, _ = o_f.shape
    hg = nseq // bsz
    tt = _pick(t_len, 512, 8)
    nt = t_len // tt
    seq = pl.BlockSpec((None, tt, LANES), lambda b, h, i: (b * hg + h, i, 0))
    tok = pl.BlockSpec((tt, LANES), lambda b, h, i: (b * nt + i, h))
    return pl.pallas_call(
        _g_readout_kernel, grid=(bsz, hg, nt),
        in_specs=[seq, seq, tok, pl.BlockSpec((1, LANES), lambda b, h, i: (0, 0))],
        out_specs=tok, out_shape=jax.ShapeDtypeStruct(z.shape, BF16),
        compiler_params=_cp("parallel", "parallel", "parallel"), name="g_readout")(o_f, o_r, z, nw)


def _merge_kernel(oa_ref, ob_ref, pa_ref, pb_ref, ga_ref, gb_ref, o_ref):
    a = _dg(oa_ref[...], pa_ref[...], _NN)
    b = _dg(ob_ref[...], pb_ref[...], _NN)
    o_ref[...] = (_sigmoid(ga_ref[...]) * a + _sigmoid(gb_ref[...]) * b).astype(o_ref.dtype)


def _merge(oa, ob, pa, pb, gt):
    m, wa = oa.shape
    wb = ob.shape[1]
    d = pa.shape[1]
    tm = _pick(m, 1024, 8)
    tn = _pick(d, 512, LANES)
    nj = d // tn
    return pl.pallas_call(
        _merge_kernel, grid=(m // tm, nj),
        in_specs=[pl.BlockSpec((tm, wa), lambda i, j: (i, 0)),
                  pl.BlockSpec((tm, wb), lambda i, j: (i, 0)),
                  pl.BlockSpec((wa, tn), lambda i, j: (0, j)),
                  pl.BlockSpec((wb, tn), lambda i, j: (0, j)),
                  pl.BlockSpec((tm, tn), lambda i, j: (i, j)),
                  pl.BlockSpec((tm, tn), lambda i, j: (i, j + nj))],
        out_specs=pl.BlockSpec((tm, tn), lambda i, j: (i, j)),
        out_shape=jax.ShapeDtypeStruct((m, d), BF16),
        compiler_params=_cp("parallel", "parallel"), name="merge")(oa, ob, pa, pb, gt, gt)


def _outproj_kernel(m_ref, w_ref, x_ref, g_ref, o_ref):
    o_ref[...] = x_ref[...] + g_ref[...] * _dg(m_ref[...], w_ref[...], _NN)


def _outproj(mm, w, x2, gate, rows_per_mod):
    m, k = mm.shape
    d = w.shape[1]
    tm = _pick(rows_per_mod, 1024, 8)
    per = rows_per_mod // tm
    tn = _pick(d, 512, LANES)
    return pl.pallas_call(
        _outproj_kernel, grid=(m // tm, d // tn),
        in_specs=[pl.BlockSpec((tm, k), lambda i, j: (i, 0)),
                  pl.BlockSpec((k, tn), lambda i, j: (0, j)),
                  pl.BlockSpec((tm, tn), lambda i, j: (i, j)),
                  pl.BlockSpec((None, 1, tn), lambda i, j: (i // per, 0, j))],
        out_specs=pl.BlockSpec((tm, tn), lambda i, j: (i, j)),
        out_shape=jax.ShapeDtypeStruct((m, d), F32),
        compiler_params=_cp("parallel", "parallel"), name="outproj")(mm, w, x2, gate)


def _ffn_up_kernel(h_ref, wg_ref, wu_ref, o_ref):
    h = h_ref[...]
    g = _dg(h, wg_ref[...], _NN)
    u = _dg(h, wu_ref[...], _NN)
    o_ref[...] = (_silu(g) * u).astype(o_ref.dtype)


def _ffn_up(h, wgu):
    m, d = h.shape
    fh = wgu.shape[1] // 2
    tm = _pick(m, 1024, 8)
    tn = _pick(fh, 512, LANES)
    nj = fh // tn
    return pl.pallas_call(
        _ffn_up_kernel, grid=(m // tm, nj),
        in_specs=[pl.BlockSpec((tm, d), lambda i, j: (i, 0)),
                  pl.BlockSpec((d, tn), lambda i, j: (0, j)),
                  pl.BlockSpec((d, tn), lambda i, j: (0, j + nj))],
        out_specs=pl.BlockSpec((tm, tn), lambda i, j: (i, j)),
        out_shape=jax.ShapeDtypeStruct((m, fh), BF16),
        compiler_params=_cp("parallel", "parallel"), name="ffn_up")(h, wgu, wgu)


def _ffn_down_kernel(a_ref, w_ref, x_ref, g_ref, fw_ref, o_ref, acc_ref):
    kk = pl.program_id(1)

    @pl.when(kk == 0)
    def _():
        acc_ref[...] = jnp.zeros_like(acc_ref)

    acc_ref[...] += _dg(a_ref[...], w_ref[...], _NN)

    @pl.when(kk == pl.num_programs(1) - 1)
    def _():
        x = x_ref[...] + g_ref[...] * acc_ref[...]
        y = x * lax.rsqrt(jnp.mean(x * x, axis=-1, keepdims=True) + NORM_EPS)
        o_ref[...] = y * fw_ref[...]


def _ffn_down(act, w, x2, gate, fw, rows_per_mod):
    m, fh = act.shape
    d = w.shape[1]
    tm = _pick(rows_per_mod, 512, 8)
    per = rows_per_mod // tm
    tk = _pick(fh, 512, LANES)
    return pl.pallas_call(
        _ffn_down_kernel, grid=(m // tm, fh // tk),
        in_specs=[pl.BlockSpec((tm, tk), lambda i, k: (i, k)),
                  pl.BlockSpec((tk, d), lambda i, k: (k, 0)),
                  pl.BlockSpec((tm, d), lambda i, k: (i, 0)),
                  pl.BlockSpec((None, 1, d), lambda i, k: (i // per, 0, 0)),
                  pl.BlockSpec((1, d), lambda i, k: (0, 0))],
        out_specs=pl.BlockSpec((tm, d), lambda i, k: (i, 0)),
        out_shape=jax.ShapeDtypeStruct((m, d), F32),
        scratch_shapes=[pltpu.VMEM((tm, d), F32)],
        compiler_params=_cp("parallel", "arbitrary"), name="ffn_down")(act, w, x2, gate, fw)


def _pad_cols(a, n):
    return jnp.pad(a, ((0, 0), (0, n - a.shape[1])))


def _roundup(n, m):
    return -(-n // m) * m


def kernel(x, c, ctx, c_ctx, w_ada, b_ada, norm1_w, w_in, rw_mu, rw_w0, rw_w2, rw_a0, rw_a2, rw_g2, rw_k_k, rw_k_a, rw_r_k, rw_ln_w, rw_ln_b, gdn_conv_w, gdn_a_log, gdn_dt_bias, gdn_norm_w, merge_p_a, merge_p_b, w_out, norm2_w, ffn_w_gate_up, ffn_w_down, final_norm_w):
    assert w_ada.shape[0] == 1, "single-layer trunk"
    bsz, t_len, d = x.shape
    tc_len = ctx.shape[1]
    h_rw, n_rw = rw_r_k.shape[1:]
    wr = h_rw * n_rw
    rk_w, rk_a, rk_g = rw_w2.shape[2], rw_a2.shape[2], rw_g2.shape[1]
    hg, dg = gdn_a_log.shape[-1], gdn_norm_w.shape[-1]
    wg = hg * dg
    assert n_rw == HALF and dg == LANES and h_rw % 2 == 0
    assert t_len % GRID_W == 0 and t_len % CHUNK == 0 and tc_len % CHUNK == 0

    w_in0 = w_in[0]
    o_wd = 3 * wr
    o_ad = o_wd + 2 * rk_w
    o_gd = o_ad + 2 * rk_a
    rw_cols = o_gd + rk_g
    wdp, adp, gdp = _roundup(2 * rk_w, LANES), _roundup(2 * rk_a, LANES), _roundup(rk_g, LANES)

    def rw_layout(a):
        return jnp.concatenate([a[:, :o_wd], _pad_cols(a[:, o_wd:o_ad], wdp), _pad_cols(a[:, o_ad:o_gd], adp),
                                _pad_cols(a[:, o_gd:rw_cols], gdp)], axis=1)

    o_cv = rw_cols
    o_z = o_cv + 3 * wg
    o_ab = o_z + wg
    o_gt = o_ab + 4 * hg
    w_rw = rw_layout(w_in0[:, :rw_cols]).astype(BF16)
    w_cv = w_in0[:, o_cv:o_z].astype(BF16)
    w_z = w_in0[:, o_z:o_ab].astype(BF16)
    w_ab = _pad_cols(w_in0[:, o_ab:o_gt], LANES).astype(BF16)
    w_gt = w_in0[:, o_gt:].astype(BF16)
    mu_p = rw_layout(rw_mu)

    w2f = jnp.zeros((wdp, 2 * wr), F32)
    a2f = jnp.zeros((adp, 2 * wr), F32)
    for dd in range(2):
        w2f = w2f.at[dd * rk_w:(dd + 1) * rk_w, dd * wr:(dd + 1) * wr].set(rw_w2[0, dd])
        a2f = a2f.at[dd * rk_a:(dd + 1) * rk_a, dd * wr:(dd + 1) * wr].set(rw_a2[0, dd])
    g2p = jnp.pad(rw_g2[0], ((0, gdp - rk_g), (0, 0)))
    li = jnp.arange(LANES)
    e64 = ((li[:, None] // HALF) == (li[None, :] // HALF)).astype(BF16)
    prm = dict(wr=wr, wdp=wdp, adp=adp,
               w0=rw_w0[0].reshape(1, 2 * wr), w2=w2f.astype(BF16),
               a0=rw_a0[0].reshape(1, 2 * wr), a2=a2f.astype(BF16), g2=g2p.astype(BF16),
               k_k=rw_k_k, k_a=rw_k_a, r_k=rw_r_k[0].reshape(1, wr), e64=e64)

    nega = _pad_cols((-jnp.exp(gdn_a_log[0])).reshape(1, 2 * hg), LANES)
    dtb = _pad_cols(gdn_dt_bias[0].reshape(1, 2 * hg), LANES)

    rows = _roundup(bsz + 1, 8)
    cc = jnp.concatenate([c, c_ctx[None, :], jnp.zeros((rows - bsz - 1, d), F32)], axis=0)
    mod = _ada(cc, w_ada[0], b_ada)
    mods = [mod[:bsz, i * d:(i + 1) * d].reshape(bsz, 1, d) for i in range(6)]
    sh1, sc1, gt1, sh2, sc2, gt2 = mods
    csh1 = mod[bsz:bsz + 1, 0:d].reshape(1, 1, d)
    csc1 = mod[bsz:bsz + 1, d:2 * d].reshape(1, 1, d)

    x2 = x.reshape(bsz * t_len, d)
    ctx2 = ctx.reshape(bsz * tc_len, d)
    h = _normmod(x2, norm1_w, sh1, sc1, t_len)
    hc = _normmod(ctx2, norm1_w, csh1, csc1, bsz * tc_len)

    def mixer_inputs(hh, seq_len, latent):
        p_rw = _matmul(hh, w_rw, name="proj_rw")
        p_cv = _matmul(hh, w_cv, name="proj_cv")
        p_ab = _matmul(hh, w_ab, name="proj_ab")
        xs = _shiftmix(p_rw.reshape(bsz, seq_len, -1), mu_p, "grid" if latent else "seq")
        feat = _rwfeat(xs.reshape(bsz * seq_len, -1), prm, latent)
        qkv = _gconv(p_cv.reshape(bsz, seq_len, -1), gdn_conv_w[0], hg, dg ** -0.5)
        gb = _gfeat(p_ab, nega, dtb, hg)
        nc = seq_len // CHUNK
        grows = gb[:, :4 * hg].reshape(bsz, nc, CHUNK, 4, hg).transpose(0, 4, 1, 3, 2)
        return feat, qkv.reshape(bsz * seq_len, -1), grows

    feat_c, qkv_c, grows_c = mixer_inputs(hc, tc_len, False)
    feat_l, qkv_l, grows_l = mixer_inputs(h, t_len, True)

    def run_mixer(prep_c, prep_l, nseq):
        s0 = jnp.zeros((2, nseq, LANES, LANES), F32)
        (s_ctx,) = _scan(prep_c[:3], prep_c[3:], s0, False)
        o_f, o_r, _ = _scan(prep_l[:5], prep_l[5:], s_ctx, True)
        return o_f, o_r

    o_f, o_r = run_mixer(_rwprep(feat_c, bsz, tc_len, False), _rwprep(feat_l, bsz, t_len, True),
                         bsz * (wr // LANES))
    o_a = _rw_readout(o_f, o_r, feat_l[7], feat_l[6], rw_ln_w, rw_ln_b, e64, bsz)

    o_f, o_r = run_mixer(_gprep(qkv_c, grows_c, bsz, tc_len, hg, False),
                         _gprep(qkv_l, grows_l, bsz, t_len, hg, True), bsz * hg)
    z = _matmul(h, w_z, name="proj_z")
    o_b = _g_readout(o_f, o_r, z, gdn_norm_w, bsz)

    gt = _matmul(h, w_gt, name="proj_gt")
    mm = _merge(o_a, o_b, merge_p_a[0].astype(BF16), merge_p_b[0].astype(BF16), gt)
    x1 = _outproj(mm, w_out[0].astype(BF16), x2, gt1, t_len)
    h2 = _normmod(x1, norm2_w, sh2, sc2, t_len)
    act = _ffn_up(h2, ffn_w_gate_up[0].astype(BF16))
    out = _ffn_down(act, ffn_w_down[0].astype(BF16), x1, gt2, final_norm_w.reshape(1, d), t_len)
    return out.reshape(bsz, t_len, d)
```

```python
import functools
import math

import jax
import jax.numpy as jnp
from jax import lax
from jax.experimental import pallas as pl
from jax.experimental.pallas import tpu as pltpu

F32 = jnp.float32
BF16 = jnp.bfloat16

GRID_W = 64
NORM_EPS = 1e-6
RW_GN_EPS = 64e-5
CHUNK = 64
LANES = 128
HALF = 64
VMEM_LIMIT = 48 * 1024 * 1024
VMEM_LIMIT_BIG = 58 * 1024 * 1024


def _pick(n, pref, mult):
    t = min(pref, n)
    t -= t % mult
    while t > mult and n % t:
        t -= mult
    assert t >= mult and n % t == 0, (n, pref, mult)
    return t


def _cp(*sem, vmem=VMEM_LIMIT):
    return pltpu.CompilerParams(dimension_semantics=sem, vmem_limit_bytes=vmem)


def _sigmoid(x):
    return 1.0 / (1.0 + jnp.exp(-x))


def _silu(x):
    return x * _sigmoid(x)


def _softplus(x):
    return jnp.maximum(x, 0.0) + jnp.log(1.0 + jnp.exp(-jnp.abs(x)))


def _split(x):
    hi = x.astype(BF16)
    lo = (x - hi.astype(F32)).astype(BF16)
    return hi, lo


_NN = (((1,), (0,)), ((), ()))
_NT = (((1,), (1,)), ((), ()))
_TN = (((0,), (0,)), ((), ()))


def _dg(a, b, dn):
    return lax.dot_general(a, b, dn, preferred_element_type=F32)


def _mm(a, b, dn=_NN, passes=1):
    if passes == 1:
        return _dg(a.astype(BF16), b.astype(BF16), dn)
    ah, al = _split(a)
    bh, bl = _split(b)
    return _dg(ah, bh, dn) + (_dg(ah, bl, dn) + _dg(al, bh, dn))


def _mm_exact_lhs(a_bf, b, dn=_NN):
    bh, bl = _split(b)
    return _dg(a_bf, bh, dn) + _dg(a_bf, bl, dn)


def _iota(shape, dim):
    return lax.broadcasted_iota(jnp.int32, shape, dim)


def _ada_kernel(c_ref, w_ref, b_ref, o_ref):
    s = _silu(c_ref[...])
    o_ref[...] = _dg(s.astype(BF16), w_ref[...].astype(BF16), _NN) + b_ref[...]


def _ada(cc, w, b):
    m, d = cc.shape
    n = w.shape[1]
    tn = _pick(n, 1024, LANES)
    return pl.pallas_call(
        _ada_kernel, grid=(n // tn,),
        in_specs=[pl.BlockSpec((m, d), lambda j: (0, 0)),
                  pl.BlockSpec((d, tn), lambda j: (0, j)),
                  pl.BlockSpec((1, tn), lambda j: (0, j))],
        out_specs=pl.BlockSpec((m, tn), lambda j: (0, j)),
        out_shape=jax.ShapeDtypeStruct((m, n), F32),
        compiler_params=_cp("parallel"), name="ada")(cc, w, b)


def _normmod_kernel(x_ref, nw_ref, sh_ref, sc_ref, o_ref):
    x = x_ref[...]
    y = x * lax.rsqrt(jnp.mean(x * x, axis=-1, keepdims=True) + NORM_EPS) * nw_ref[...]
    o_ref[...] = (y * (1.0 + sc_ref[...]) + sh_ref[...]).astype(o_ref.dtype)


def _normmod(x2, nw, sh, sc, rows_per_mod):
    m, d = x2.shape
    tm = _pick(rows_per_mod, 512, 8)
    per = rows_per_mod // tm
    return pl.pallas_call(
        _normmod_kernel, grid=(m // tm,),
        in_specs=[pl.BlockSpec((tm, d), lambda i: (i, 0)),
                  pl.BlockSpec((1, d), lambda i: (0, 0)),
                  pl.BlockSpec((None, 1, d), lambda i: (i // per, 0, 0)),
                  pl.BlockSpec((None, 1, d), lambda i: (i // per, 0, 0))],
        out_specs=pl.BlockSpec((tm, d), lambda i: (i, 0)),
        out_shape=jax.ShapeDtypeStruct((m, d), BF16),
        compiler_params=_cp("parallel"), name="normmod")(x2, nw, sh, sc)


def _mm_kernel(a_ref, w_ref, o_ref):
    o_ref[...] = _dg(a_ref[...], w_ref[...], _NN).astype(o_ref.dtype)


def _matmul(a, w, out_dtype=F32, tm=1024, tn=512, name="matmul"):
    m, k = a.shape
    n = w.shape[1]
    tm = _pick(m, tm, 8)
    tn = _pick(n, tn, LANES)
    return pl.pallas_call(
        _mm_kernel, grid=(m // tm, n // tn),
        in_specs=[pl.BlockSpec((tm, k), lambda i, j: (i, 0)),
                  pl.BlockSpec((k, tn), lambda i, j: (0, j))],
        out_specs=pl.BlockSpec((tm, tn), lambda i, j: (i, j)),
        out_shape=jax.ShapeDtypeStruct((m, n), out_dtype),
        compiler_params=_cp("parallel", "parallel"), name=name)(a, w)


def _shiftmix_kernel(mode, t_len, tt, p_ref, prev_ref, next_ref, mu_ref, o_ref):
    i = pl.program_id(1)
    p = p_ref[...]
    shp = p.shape
    t = i * tt + _iota(shp, 0)
    lane = _iota(shp, 1)
    if mode == "grid":
        cl = lane & 3
        w = t & (GRID_W - 1)
        left = jnp.where(w != 0, pltpu.roll(p, 1, 0), 0.0)
        right = jnp.where(w != GRID_W - 1, pltpu.roll(p, tt - 1, 0), 0.0)
        if tt > GRID_W:
            up = jnp.concatenate([prev_ref[...], p[:tt - GRID_W]], axis=0)
            down = jnp.concatenate([p[GRID_W:], next_ref[...]], axis=0)
        else:
            up, down = prev_ref[...], next_ref[...]
        up = jnp.where(t >= GRID_W, up, 0.0)
        down = jnp.where(t < t_len - GRID_W, down, 0.0)
        sh = jnp.where(cl == 0, left, jnp.where(cl == 1, right, jnp.where(cl == 2, up, down)))
    else:
        prv = jnp.where(t != 0, pltpu.roll(p, 1, 0), 0.0)
        nxt = jnp.where(t != t_len - 1, pltpu.roll(p, tt - 1, 0), 0.0)
        sh = jnp.where((lane & 1) == 0, prv, nxt)
    o_ref[...] = p + (sh - p) * mu_ref[...]


def _shiftmix(p3, mu, mode):
    b, t_len, cp = p3.shape
    cb = _pick(cp, 512, LANES)
    tt = _pick(t_len, 512, GRID_W) if mode == "grid" else t_len
    hb = tt // GRID_W if mode == "grid" else 1
    nh = t_len // GRID_W
    halo = GRID_W if mode == "grid" else 8
    kern = functools.partial(_shiftmix_kernel, mode, t_len, tt)
    return pl.pallas_call(
        kern, grid=(b, t_len // tt, cp // cb),
        in_specs=[pl.BlockSpec((None, tt, cb), lambda bi, i, c: (bi, i, c)),
                  pl.BlockSpec((None, halo, cb), lambda bi, i, c: (bi, jnp.maximum(i * hb - 1, 0), c)),
                  pl.BlockSpec((None, halo, cb), lambda bi, i, c: (bi, jnp.minimum((i + 1) * hb, nh - 1), c)),
                  pl.BlockSpec((1, cb), lambda bi, i, c: (0, c))],
        out_specs=pl.BlockSpec((None, tt, cb), lambda bi, i, c: (bi, i, c)),
        out_shape=jax.ShapeDtypeStruct(p3.shape, F32),
        compiler_params=_cp("parallel", "parallel", "parallel"), name="shiftmix")(p3, p3, p3, mu)


def _segsum(x, e):
    outs = []
    for g in range(x.shape[1] // LANES):
        outs.append(_mm_exact_lhs_rhs(x[:, g * LANES:(g + 1) * LANES], e))
    return outs[0] if len(outs) == 1 else jnp.concatenate(outs, axis=1)


def _mm_exact_lhs_rhs(x, e_bf):
    hi, lo = _split(x)
    return _dg(hi, e_bf, _NN) + _dg(lo, e_bf, _NN)


def _rwfeat_kernel(wr, wdp, adp, latent, xs_ref, w0_ref, w2_ref, a0_ref, a2_ref, g2_ref, kk_ref_, ka_ref,
                   rk_ref, e_ref, *outs):
    if latent:
        r_o, v_o, kk_o, lw_o, kd_o, bd_o, gate_o, bonus_o = outs
    else:
        r_o, v_o, kk_o, lw_o, kd_o, bd_o = outs
    e = e_ref[...]
    r = xs_ref[:, 0:wr]
    k = xs_ref[:, wr:2 * wr]
    v = xs_ref[:, 2 * wr:3 * wr]
    wd = xs_ref[:, 3 * wr:3 * wr + wdp]
    ad = xs_ref[:, 3 * wr + wdp:3 * wr + wdp + adp]
    lw = _dg(jnp.tanh(wd).astype(BF16), w2_ref[...], _NN) + w0_ref[...]
    w_log = -_softplus(-lw) - 0.5
    logw = -jnp.exp(w_log)
    iclr = _sigmoid(_dg(ad.astype(BF16), a2_ref[...], _NN) + a0_ref[...])
    kk0 = k * kk_ref_[...]
    kk = kk0 * lax.rsqrt(_segsum(kk0 * kk0, e) + NORM_EPS)
    r_o[...] = r
    v_o[...] = v
    kk_o[...] = kk
    ka = ka_ref[...]
    kdirs = []
    for d in range(2):
        ic = iclr[:, d * wr:(d + 1) * wr]
        kd = k * (1.0 + (ic - 1.0) * ka)
        kdirs.append(kd)
        lw_o[d] = logw[:, d * wr:(d + 1) * wr]
        kd_o[d] = kd
        bd_o[d] = kk * ic
    if latent:
        gd = xs_ref[:, 3 * wr + wdp + adp:]
        gate_o[...] = _dg(_sigmoid(gd).astype(BF16), g2_ref[...], _NN)
        kb = 0.5 * (kdirs[0] + kdirs[1])
        bonus_o[...] = _segsum(r * kb * rk_ref[...], e) * v


def _rwfeat(xs, prm, latent):
    m, cp = xs.shape
    wr, wdp, adp = prm["wr"], prm["wdp"], prm["adp"]
    tm = _pick(m, 128, 8)
    row = lambda i: (i, 0)
    full = lambda i: (0, 0)
    w_specs = [pl.BlockSpec(prm[n].shape, full) for n in
               ("w0", "w2", "a0", "a2", "g2", "k_k", "k_a", "r_k", "e64")]
    tok = pl.BlockSpec((tm, wr), row)
    tok2 = pl.BlockSpec((2, tm, wr), lambda i: (0, i, 0))
    s1 = jax.ShapeDtypeStruct((m, wr), F32)
    s2 = jax.ShapeDtypeStruct((2, m, wr), F32)
    out_specs = [tok, tok, tok, tok2, tok2, tok2]
    out_shape = [s1, s1, s1, s2, s2, s2]
    if latent:
        out_specs += [tok, tok]
        out_shape += [s1, s1]
    kern = functools.partial(_rwfeat_kernel, wr, wdp, adp, latent)
    return pl.pallas_call(
        kern, grid=(m // tm,),
        in_specs=[pl.BlockSpec((tm, cp), row)] + w_specs,
        out_specs=out_specs, out_shape=out_shape,
        compiler_params=_cp("parallel"), name="rwfeat")(
            xs, *[prm[n] for n in ("w0", "w2", "a0", "a2", "g2", "k_k", "k_a", "r_k", "e64")])


INV_PASSES = 1
PREP_PASSES = 1
PREP_UNROLL = 8


def _prep_unroll(tc):
    u = PREP_UNROLL
    while tc % u:
        u //= 2
    return u


def _stack(x, h0):
    return jnp.concatenate([jnp.where(h0, x, 0.0), jnp.where(h0, 0.0, x)], axis=0)


def _tri_inverse_many(mats, eye2, h0):
    c = CHUNK
    steps = int(math.log2(c)) - 1
    ps = [eye2 - a for a in mats]
    xs = [_mm(a, _stack(a, h0), passes=INV_PASSES) for a in mats]
    for i in range(steps):
        last = i == steps - 1
        for n in range(len(mats)):
            rhs = _stack(xs[n], h0)
            if last:
                ps[n] = ps[n] + _mm(ps[n], rhs, passes=INV_PASSES)
            else:
                both = _mm(jnp.concatenate([ps[n], xs[n]], axis=0), rhs, passes=INV_PASSES)
                ps[n] = ps[n] + both[:c]
                xs[n] = both[c:]
    return ps


def _rwprep_chunks(need_out, insts):
    c = CHUNK
    shp = (c, LANES)
    row = _iota(shp, 0)
    lane = _iota(shp, 1)
    col = lane & (HALF - 1)
    h0 = lane < HALF
    eye2 = jnp.where(col == row, 1.0, 0.0)
    bdm = (_iota((LANES, LANES), 0) < HALF) == (_iota((LANES, LANES), 1) < HALF)
    cc_r, cc_c = _iota((c, c), 0), _iota((c, c), 1)
    tri_bf = {False: jnp.where(cc_c <= cc_r, 1.0, 0.0).astype(BF16),
              True: jnp.where(cc_c >= cc_r, 1.0, 0.0).astype(BF16)}
    strict = {False: col < row, True: col > row}
    incl = {False: col <= row, True: col >= row}
    n = len(insts)
    gs = [_mm_exact_lhs(tri_bf[rev], lw) for (rev, r, v, kk, lw, k, bb) in insts]
    st = []
    for (rev, r, v, kk, lw, k, bb), g in zip(insts, gs):
        last, mid = (0, c // 2) if rev else (c - 1, c // 2 - 1)
        gm = g - lw
        gtot = g[last:last + 1, :]
        rho = g[mid:mid + 1, :]
        e_inv = jnp.exp(rho - g)
        kkd = kk * jnp.exp(gm - rho)
        et = jnp.exp(gtot - g)
        d = dict(rev=rev, r=r, v=v, g=g, kt=k * et, bbt=bb * et, pc=jnp.exp(gtot), kkd0=kk * jnp.exp(gm))
        lhs = jnp.concatenate([kkd, r * jnp.exp(g - rho)], axis=0) if need_out else kkd
        rhs = jnp.concatenate([_stack(bb * e_inv, h0), _stack(k * e_inv, h0)], axis=0)
        d["ab"] = _mm(lhs, rhs, _NT, PREP_PASSES)
        st.append(d)
    for d in st:
        ab = d["ab"]
        d["a"] = jnp.where(strict[d["rev"]], ab[:c, :LANES], 0.0)
        d["sv"] = _stack(d["v"], h0)
        d["bv"] = _mm(jnp.where(strict[d["rev"]], ab[:c, LANES:], 0.0), d["sv"], passes=PREP_PASSES)
    ts = _tri_inverse_many([d["a"] for d in st], eye2, h0)
    for d, t in zip(st, ts):
        wu = _mm(t, jnp.concatenate([_stack(d["kkd0"], h0), _stack(d["bv"], h0)], axis=1), passes=PREP_PASSES)
        d["w"], d["u0"] = wu[:, :LANES], wu[:, LANES:]
    out = []
    for d in st:
        w, u0, bbt = d["w"], d["u0"], d["bbt"]
        mn = jnp.where(bdm, _mm(bbt, w, _TN, PREP_PASSES), 0.0)
        nnt = jnp.where(bdm, _mm(d["v"], d["kt"], _TN, PREP_PASSES) - _mm(u0, bbt, _TN, PREP_PASSES), 0.0)
        if not need_out:
            out.append((None, None, mn, nnt, d["pc"]))
            continue
        ab = d["ab"]
        ar = jnp.where(incl[d["rev"]], ab[c:, :LANES], 0.0)
        br = jnp.where(incl[d["rev"]], ab[c:, LANES:], 0.0)
        arwu = _mm(ar, jnp.concatenate([_stack(w, h0), _stack(u0, h0)], axis=1), passes=PREP_PASSES)
        qp = d["r"] * jnp.exp(d["g"]) - arwu[:, :LANES]
        o0 = _mm(br, d["sv"], passes=PREP_PASSES) - arwu[:, LANES:]
        out.append((qp, o0, mn, nnt, d["pc"]))
    return out


def _rwprep_kernel(tc, need_out, r_ref, v_ref, kk_ref, lwf_ref, lwr_ref, kf_ref, kr_ref, bf_ref, br_ref, *outs):
    per = 5 if need_out else 3
    outs_d = (outs[:per], outs[per:])
    ins_d = ((lwf_ref, kf_ref, bf_ref), (lwr_ref, kr_ref, br_ref))

    unroll = _prep_unroll(tc)

    def body(it, carry):
        insts, where = [], []
        for u in range(unroll):
            ci = it * unroll + u
            sl = pl.ds(pl.multiple_of(ci * CHUNK, CHUNK), CHUNK)
            r, v, kk = r_ref[sl, :], v_ref[sl, :], kk_ref[sl, :]
            for d in range(2):
                lw_ref, k_ref, b_ref = ins_d[d]
                insts.append((d == 1, r, v, kk, lw_ref[sl, :], k_ref[sl, :], b_ref[sl, :]))
                where.append((d, ci, sl))
        for (d, ci, sl), (qp, o0, mn, nnt, pc) in zip(where, _rwprep_chunks(need_out, insts)):
            o = outs_d[d]
            if need_out:
                o[0][sl, :] = qp.astype(BF16)
                o[1][sl, :] = o0
                o = o[2:]
            o[0][ci] = mn.astype(BF16)
            o[1][ci] = nnt
            o[2][pl.ds(ci, 1), :] = pc
        return carry

    lax.fori_loop(0, tc // unroll, body, 0)


def _prep_out(nseq, t_len, need_out):
    nc = t_len // CHUNK
    shapes = []
    if need_out:
        shapes += [jax.ShapeDtypeStruct((nseq, t_len, LANES), BF16), jax.ShapeDtypeStruct((nseq, t_len, LANES), F32)]
    shapes += [jax.ShapeDtypeStruct((nseq, nc, LANES, LANES), BF16),
               jax.ShapeDtypeStruct((nseq, nc, LANES, LANES), F32)]
    shapes += [jax.ShapeDtypeStruct((nseq, nc, LANES), F32)]
    return shapes


def _prep_out_specs(tc, need_out, seq_of):
    specs = []
    if need_out:
        specs += [pl.BlockSpec((None, tc * CHUNK, LANES), lambda b, p, j: (seq_of(b, p), j, 0))] * 2
    specs += [pl.BlockSpec((None, tc, LANES, LANES), lambda b, p, j: (seq_of(b, p), j, 0, 0))] * 2
    specs += [pl.BlockSpec((None, tc, LANES), lambda b, p, j: (seq_of(b, p), j, 0))]
    return specs


def _chunks_per_step(nc):
    return 8 if nc % 8 == 0 else nc


def _rwprep(feat, bsz, t_len, need_out):
    r, v, kk, lw, kd, bd = feat[:6]
    wr = r.shape[1]
    npair = wr // LANES
    nc = t_len // CHUNK
    tc = _chunks_per_step(nc)
    nblk = nc // tc
    tok = pl.BlockSpec((tc * CHUNK, LANES), lambda b, p, j: (b * nblk + j, p))
    tok_d = lambda d: pl.BlockSpec((None, tc * CHUNK, LANES), lambda b, p, j: (d, b * nblk + j, p))
    seq_of = lambda b, p: b * npair + p
    kern = functools.partial(_rwprep_kernel, tc, need_out)
    return pl.pallas_call(
        kern, grid=(bsz, npair, nblk),
        in_specs=[tok, tok, tok, tok_d(0), tok_d(1), tok_d(0), tok_d(1), tok_d(0), tok_d(1)],
        out_specs=_prep_out_specs(tc, need_out, seq_of) * 2,
        out_shape=_prep_out(bsz * npair, t_len, need_out) * 2,
        compiler_params=_cp("parallel", "parallel", "parallel"), name="rwprep")(
            r, v, kk, lw, lw, kd, kd, bd, bd)


SCAN_GROUP = 8
SCAN_CHUNKS = 4


def _scan_kernel(tc, ng, need_out, *refs):
    per = 5 if need_out else 3
    ins = (refs[:per], refs[per:2 * per])
    s0_ref = refs[2 * per]
    rest = refs[2 * per + 1:]
    if need_out:
        o_refs = rest[:2]
        rest = rest[2:]
    sfin_ref, s_scr = rest
    j = pl.program_id(1)

    @pl.when(j == 0)
    def _():
        s_scr[...] = s0_ref[...]

    def body(i, carry):
        for g in range(ng):
            for d in range(2):
                s = s_scr[d, g]
                s_bf = s.astype(BF16)
                ci = i if d == 0 else tc - 1 - i
                refs_d = ins[d]
                if need_out:
                    sl = pl.ds(pl.multiple_of(ci * CHUNK, CHUNK), CHUNK)
                    o_refs[d][g, sl, :] = _dg(refs_d[0][g, sl, :], s_bf, _NT) + refs_d[1][g, sl, :]
                    refs_d = refs_d[2:]
                upd = _dg(s_bf, refs_d[0][g, ci], _NT)
                s_scr[d, g] = s * refs_d[2][g, ci] - upd + refs_d[1][g, ci]
        return carry

    lax.fori_loop(0, tc, body, 0)

    @pl.when(j == pl.num_programs(1) - 1)
    def _():
        sfin_ref[...] = s_scr[...]


def _scan(prep_f, prep_r, s0, need_out):
    mn = prep_f[-3]
    nseq, nc = mn.shape[0], mn.shape[1]
    tc = _pick(nc, SCAN_CHUNKS, 1)
    nblk = nc // tc
    ng = _pick(nseq, SCAN_GROUP, 1)
    prep_f = list(prep_f[:-1]) + [prep_f[-1].reshape(nseq, nc, 1, LANES)]
    prep_r = list(prep_r[:-1]) + [prep_r[-1].reshape(nseq, nc, 1, LANES)]

    def specs(rev):
        jj = (lambda j: nblk - 1 - j) if rev else (lambda j: j)
        sp = []
        if need_out:
            sp += [pl.BlockSpec((ng, tc * CHUNK, LANES), lambda s, j: (s, jj(j), 0))] * 2
        sp += [pl.BlockSpec((ng, tc, LANES, LANES), lambda s, j: (s, jj(j), 0, 0))] * 2
        sp += [pl.BlockSpec((ng, tc, 1, LANES), lambda s, j: (s, jj(j), 0, 0))]
        return sp

    st_spec = pl.BlockSpec((2, ng, LANES, LANES), lambda s, j: (0, s, 0, 0))
    out_specs, out_shape = [], []
    if need_out:
        out_specs += [specs(False)[0], specs(True)[0]]
        out_shape += [jax.ShapeDtypeStruct((nseq, nc * CHUNK, LANES), F32)] * 2
    out_specs.append(st_spec)
    out_shape.append(jax.ShapeDtypeStruct((2, nseq, LANES, LANES), F32))
    kern = functools.partial(_scan_kernel, tc, ng, need_out)
    return pl.pallas_call(
        kern, grid=(nseq // ng, nblk),
        in_specs=specs(False) + specs(True) + [st_spec],
        out_specs=out_specs, out_shape=out_shape,
        scratch_shapes=[pltpu.VMEM((2, ng, LANES, LANES), F32)],
        compiler_params=_cp("parallel", "arbitrary"), name="scan")(*prep_f, *prep_r, s0)


READOUT_GROUP = 4


def _rw_readout_kernel(of_ref, or_ref, bonus_ref, gate_ref, lnw_ref, lnb_ref, e_ref, o_ref):
    e = e_ref[...]
    inv = 1.0 / HALF
    for g in range(of_ref.shape[0]):
        cs = slice(g * LANES, (g + 1) * LANES)
        y = of_ref[g] + or_ref[g]
        mu = _segsum(y, e) * inv
        yc = y - mu
        var = _segsum(yc * yc, e) * inv
        yn = yc * lax.rsqrt(var + RW_GN_EPS) * lnw_ref[:, cs] + lnb_ref[:, cs]
        o_ref[:, cs] = ((yn + bonus_ref[:, cs]) * gate_ref[:, cs]).astype(o_ref.dtype)


def _rw_readout(o_f, o_r, bonus, gate, lnw, lnb, e64, bsz):
    nseq, t_len, _ = o_f.shape
    npair = nseq // bsz
    ng = _pick(npair, READOUT_GROUP, 1)
    npg = npair // ng
    tt = _pick(t_len, 512, 8)
    nt = t_len // tt
    seq = pl.BlockSpec((ng, tt, LANES), lambda b, p, i: (b * npg + p, i, 0))
    tok = pl.BlockSpec((tt, ng * LANES), lambda b, p, i: (b * nt + i, p))
    par = pl.BlockSpec((1, ng * LANES), lambda b, p, i: (0, p))
    return pl.pallas_call(
        _rw_readout_kernel, grid=(bsz, npg, nt),
        in_specs=[seq, seq, tok, tok, par, par, pl.BlockSpec((LANES, LANES), lambda b, p, i: (0, 0))],
        out_specs=tok, out_shape=jax.ShapeDtypeStruct(bonus.shape, BF16),
        compiler_params=_cp("parallel", "parallel", "parallel"), name="rw_readout")(
            o_f, o_r, bonus, gate, lnw, lnb, e64)


def _gconv_kernel(taps, nq, scale, p_ref, prev_ref, next_ref, w_ref, o_ref):
    i = pl.program_id(1)
    cb = pl.program_id(2)
    main = p_ref[...]
    tt = main.shape[0]
    prev = jnp.where(i > 0, prev_ref[...], 0.0)
    nxt = jnp.where(i < pl.num_programs(1) - 1, next_ref[...], 0.0)
    xcat = jnp.concatenate([prev, main, nxt], axis=0)
    n = tt + 16
    half = taps // 2
    acc = None
    for j in range(taps):
        sh = (half - j) % n
        xs = xcat if sh == 0 else pltpu.roll(xcat, sh, 0)
        term = xs[8:8 + tt] * w_ref[j:j + 1, :]
        acc = term if acc is None else acc + term
    u = _silu(acc)
    heads = u.shape[1] // LANES
    for s in range(heads):
        hd = cb * heads + s
        us = u[:, s * LANES:(s + 1) * LANES]
        un = us * lax.rsqrt(jnp.sum(us * us, axis=-1, keepdims=True) + NORM_EPS)
        un = un * jnp.where(hd < nq, scale, 1.0)
        o_ref[:, s * LANES:(s + 1) * LANES] = jnp.where(hd < 2 * nq, un, us)


def _gconv(p3, conv_w, nq, scale):
    b, t_len, c3 = p3.shape
    taps = conv_w.shape[0]
    tt = _pick(t_len, 512, 8)
    hb = tt // 8
    nh = t_len // 8
    kern = functools.partial(_gconv_kernel, taps, nq, scale)
    cb = _pick(c3, 512, LANES)
    return pl.pallas_call(
        kern, grid=(b, t_len // tt, c3 // cb),
        in_specs=[pl.BlockSpec((None, tt, cb), lambda bi, i, c: (bi, i, c)),
                  pl.BlockSpec((None, 8, cb), lambda bi, i, c: (bi, jnp.maximum(i * hb - 1, 0), c)),
                  pl.BlockSpec((None, 8, cb), lambda bi, i, c: (bi, jnp.minimum((i + 1) * hb, nh - 1), c)),
                  pl.BlockSpec((taps, cb), lambda bi, i, c: (0, c))],
        out_specs=pl.BlockSpec((None, tt, cb), lambda bi, i, c: (bi, i, c)),
        out_shape=jax.ShapeDtypeStruct(p3.shape, F32),
        compiler_params=_cp("parallel", "parallel", "parallel"), name="gconv")(p3, p3, p3, conv_w)


def _gfeat_kernel(hg, ab_ref, nega_ref, dtb_ref, o_ref):
    ab = ab_ref[...]
    tm = ab.shape[0]
    lane = _iota(ab.shape, 1)
    g = jnp.where(lane < 2 * hg, nega_ref[...] * _softplus(ab + dtb_ref[...]), 0.0)
    beta = _sigmoid(ab)
    rr = _iota((tm, tm), 0)
    cc = _iota((tm, tm), 1)
    same = (rr // CHUNK) == (cc // CHUNK)
    lf = jnp.where(same & (cc <= rr), 1.0, 0.0).astype(BF16)
    lr = jnp.where(same & (cc >= rr), 1.0, 0.0).astype(BF16)
    gcf = _mm_exact_lhs(lf, g)
    gcr = _mm_exact_lhs(lr, g)
    gc = jnp.where(lane < hg, gcf, gcr)
    o_ref[...] = jnp.where(lane < 2 * hg, gc, beta)


def _gfeat(ab, nega, dtb, hg):
    m = ab.shape[0]
    tm = _pick(m, 256, CHUNK)
    return pl.pallas_call(
        functools.partial(_gfeat_kernel, hg), grid=(m // tm,),
        in_specs=[pl.BlockSpec((tm, LANES), lambda i: (i, 0)),
                  pl.BlockSpec((1, LANES), lambda i: (0, 0)),
                  pl.BlockSpec((1, LANES), lambda i: (0, 0))],
        out_specs=pl.BlockSpec((tm, LANES), lambda i: (i, 0)),
        out_shape=jax.ShapeDtypeStruct((m, LANES), F32),
        compiler_params=_cp("parallel"), name="gfeat")(ab, nega, dtb)


def _col_from_row(rowv, eye):
    c = rowv.shape[1]
    return jnp.sum(jnp.where(eye, jnp.broadcast_to(rowv, (c, c)), 0.0), axis=1, keepdims=True)


def _gprep_chunks(need_out, insts):
    c = CHUNK
    shp = (c, LANES)
    row = _iota(shp, 0)
    lane = _iota(shp, 1)
    col = lane & (HALF - 1)
    f0 = lane < HALF
    eye = _iota((c, c), 0) == _iota((c, c), 1)
    ahead = jnp.where(f0, row - col, col - row)
    strict = ahead > 0
    incl = ahead >= 0
    eye2 = jnp.where(col == row, 1.0, 0.0)
    st = []
    for (q, k, v, rows) in insts:
        kk2 = jnp.concatenate([k, k], axis=0)
        lhs = jnp.concatenate([k, q], axis=0) if need_out else k
        st.append(dict(q=q, k=k, v=v, rows=rows, kq=_mm(lhs, kk2, _NT, PREP_PASSES)))
    for d in st:
        rows = d["rows"]
        gr = (rows[0:1, :], rows[1:2, :])
        gcol = [_col_from_row(x, eye) for x in gr]
        bcol = [_col_from_row(rows[2 + dd:3 + dd, :], eye) for dd in range(2)]
        grow2 = jnp.concatenate(gr, axis=1)
        gcol2 = jnp.where(f0, gcol[0], gcol[1])
        bcol2 = jnp.where(f0, bcol[0], bcol[1])
        decay = jnp.where(incl, jnp.exp(jnp.where(incl, gcol2 - grow2, 0.0)), 0.0)
        d.update(gcol=gcol, bcol=bcol, decay=decay, glast=(gr[0][:, c - 1:c], gr[1][:, 0:1]),
                 lmat=jnp.where(strict, d["kq"][:c] * bcol2 * decay, 0.0))
    ts = _tri_inverse_many([d["lmat"] for d in st], eye2, f0)
    for d, t in zip(st, ts):
        k, v, gcol, bcol = d["k"], d["v"], d["gcol"], d["bcol"]
        egc = [jnp.exp(gcol[dd]) for dd in range(2)]
        rhs2 = jnp.concatenate([jnp.concatenate([v * bcol[dd], k * (bcol[dd] * egc[dd])], axis=1)
                                for dd in range(2)], axis=0)
        d["egc"] = egc
        d["sol"] = [_mm(jnp.where(f0, t, 0.0), rhs2, passes=PREP_PASSES),
                    _mm(jnp.where(f0, 0.0, t), rhs2, passes=PREP_PASSES)]
    res = []
    for d in st:
        sol, k, q = d["sol"], d["k"], d["q"]
        if need_out:
            qk = jnp.where(incl, d["kq"][c:] * d["decay"], 0.0)
            sol2 = jnp.concatenate(sol, axis=0)
            xx = [_mm(jnp.where(f0, qk, 0.0), sol2, passes=PREP_PASSES),
                  _mm(jnp.where(f0, 0.0, qk), sol2, passes=PREP_PASSES)]
        out = []
        for dd in range(2):
            u = sol[dd][:, :LANES]
            wk = sol[dd][:, LANES:]
            ktail = k * jnp.exp(d["glast"][dd] - d["gcol"][dd])
            mn = _mm(ktail, wk, _TN, PREP_PASSES)
            nnt = _mm(u, ktail, _TN, PREP_PASSES)
            pc = jnp.broadcast_to(jnp.exp(d["glast"][dd]), (1, LANES))
            if need_out:
                out.append((q * d["egc"][dd] - xx[dd][:, LANES:], xx[dd][:, :LANES], mn, nnt, pc))
            else:
                out.append((None, None, mn, nnt, pc))
        res.append(out)
    return res


def _gprep_kernel(tc, need_out, q_ref, k_ref, v_ref, rows_ref, *outs):
    per = 5 if need_out else 3
    outs_d = (outs[:per], outs[per:])

    unroll = _prep_unroll(tc)

    def body(it, carry):
        insts, where = [], []
        for u in range(unroll):
            ci = it * unroll + u
            sl = pl.ds(pl.multiple_of(ci * CHUNK, CHUNK), CHUNK)
            q = q_ref[sl, :] if need_out else None
            insts.append((q, k_ref[sl, :], v_ref[sl, :], rows_ref[ci]))
            where.append((ci, sl))
        for (ci, sl), res in zip(where, _gprep_chunks(need_out, insts)):
            for d in range(2):
                qp, o0, mn, nnt, pc = res[d]
                o = outs_d[d]
                if need_out:
                    o[0][sl, :] = qp.astype(BF16)
                    o[1][sl, :] = o0
                    o = o[2:]
                o[0][ci] = mn.astype(BF16)
                o[1][ci] = nnt
                o[2][pl.ds(ci, 1), :] = pc
        return carry

    lax.fori_loop(0, tc // unroll, body, 0)


def _gprep(qkv, rows, bsz, t_len, hg, need_out):
    nc = t_len // CHUNK
    tc = _chunks_per_step(nc)
    nblk = nc // tc
    tok = lambda off: pl.BlockSpec((tc * CHUNK, LANES), lambda b, h, j: (b * nblk + j, off + h))
    seq_of = lambda b, h: b * hg + h
    kern = functools.partial(_gprep_kernel, tc, need_out)
    return pl.pallas_call(
        kern, grid=(bsz, hg, nblk),
        in_specs=[tok(0), tok(hg), tok(2 * hg),
                  pl.BlockSpec((None, None, tc, 4, CHUNK), lambda b, h, j: (b, h, j, 0, 0))],
        out_specs=_prep_out_specs(tc, need_out, seq_of) * 2,
        out_shape=_prep_out(bsz * hg, t_len, need_out) * 2,
        compiler_params=_cp("parallel", "parallel", "parallel"), name="gprep")(qkv, qkv, qkv, rows)


def _g_readout_kernel(of_ref, or_ref, z_ref, nw_ref, o_ref):
    for g in range(of_ref.shape[0]):
        cs = slice(g * LANES, (g + 1) * LANES)
        o = of_ref[g] + or_ref[g]
        on = o * lax.rsqrt(jnp.mean(o * o, axis=-1, keepdims=True) + NORM_EPS) * nw_ref[...]
        o_ref[:, cs] = (on * _silu(z_ref[:, cs])).astype(o_ref.dtype)


def _g_readout(o_f, o_r, z, nw, bsz):
    nseq, t_len, _ = o_f.shape
    hg = nseq // bsz
    ng = _pick(hg, READOUT_GROUP, 1)
    npg = hg // ng
    tt = _pick(t_len, 512, 8)
    nt = t_len // tt
    seq = pl.BlockSpec((ng, tt, LANES), lambda b, h, i: (b * npg + h, i, 0))
    tok = pl.BlockSpec((tt, ng * LANES), lambda b, h, i: (b * nt + i, h))
    return pl.pallas_call(
        _g_readout_kernel, grid=(bsz, npg, nt),
        in_specs=[seq, seq, tok, pl.BlockSpec((1, LANES), lambda b, h, i: (0, 0))],
        out_specs=tok, out_shape=jax.ShapeDtypeStruct(z.shape, BF16),
        compiler_params=_cp("parallel", "parallel", "parallel"), name="g_readout")(o_f, o_r, z, nw)


def _merge_kernel(oa_ref, ob_ref, pa_ref, pb_ref, ga_ref, gb_ref, o_ref):
    a = _dg(oa_ref[...], pa_ref[...], _NN)
    b = _dg(ob_ref[...], pb_ref[...], _NN)
    o_ref[...] = (_sigmoid(ga_ref[...]) * a + _sigmoid(gb_ref[...]) * b).astype(o_ref.dtype)


def _merge(oa, ob, pa, pb, gt):
    m, wa = oa.shape
    wb = ob.shape[1]
    d = pa.shape[1]
    tm = _pick(m, 1024, 8)
    tn = _pick(d, 512, LANES)
    nj = d // tn
    return pl.pallas_call(
        _merge_kernel, grid=(m // tm, nj),
        in_specs=[pl.BlockSpec((tm, wa), lambda i, j: (i, 0)),
                  pl.BlockSpec((tm, wb), lambda i, j: (i, 0)),
                  pl.BlockSpec((wa, tn), lambda i, j: (0, j)),
                  pl.BlockSpec((wb, tn), lambda i, j: (0, j)),
                  pl.BlockSpec((tm, tn), lambda i, j: (i, j)),
                  pl.BlockSpec((tm, tn), lambda i, j: (i, j + nj))],
        out_specs=pl.BlockSpec((tm, tn), lambda i, j: (i, j)),
        out_shape=jax.ShapeDtypeStruct((m, d), BF16),
        compiler_params=_cp("parallel", "parallel"), name="merge")(oa, ob, pa, pb, gt, gt)


def _outproj_kernel(m_ref, w_ref, x_ref, g_ref, o_ref):
    o_ref[...] = x_ref[...] + g_ref[...] * _dg(m_ref[...], w_ref[...], _NN)


def _outproj(mm, w, x2, gate, rows_per_mod):
    m, k = mm.shape
    d = w.shape[1]
    tm = _pick(rows_per_mod, 1024, 8)
    per = rows_per_mod // tm
    tn = _pick(d, 512, LANES)
    return pl.pallas_call(
        _outproj_kernel, grid=(m // tm, d // tn),
        in_specs=[pl.BlockSpec((tm, k), lambda i, j: (i, 0)),
                  pl.BlockSpec((k, tn), lambda i, j: (0, j)),
                  pl.BlockSpec((tm, tn), lambda i, j: (i, j)),
                  pl.BlockSpec((None, 1, tn), lambda i, j: (i // per, 0, j))],
        out_specs=pl.BlockSpec((tm, tn), lambda i, j: (i, j)),
        out_shape=jax.ShapeDtypeStruct((m, d), F32),
        compiler_params=_cp("parallel", "parallel"), name="outproj")(mm, w, x2, gate)


def _ffn_up_kernel(h_ref, wg_ref, wu_ref, o_ref):
    h = h_ref[...]
    g = _dg(h, wg_ref[...], _NN)
    u = _dg(h, wu_ref[...], _NN)
    o_ref[...] = (_silu(g) * u).astype(o_ref.dtype)


def _ffn_up(h, wgu):
    m, d = h.shape
    fh = wgu.shape[1] // 2
    tm = _pick(m, 1024, 8)
    tn = _pick(fh, 512, LANES)
    nj = fh // tn
    return pl.pallas_call(
        _ffn_up_kernel, grid=(m // tm, nj),
        in_specs=[pl.BlockSpec((tm, d), lambda i, j: (i, 0)),
                  pl.BlockSpec((d, tn), lambda i, j: (0, j)),
                  pl.BlockSpec((d, tn), lambda i, j: (0, j + nj))],
        out_specs=pl.BlockSpec((tm, tn), lambda i, j: (i, j)),
        out_shape=jax.ShapeDtypeStruct((m, fh), BF16),
        compiler_params=_cp("parallel", "parallel"), name="ffn_up")(h, wgu, wgu)


def _ffn_down_kernel(a_ref, w_ref, x_ref, g_ref, fw_ref, o_ref):
    kk = pl.program_id(1)
    part = _dg(a_ref[...], w_ref[...], _NN)

    @pl.when(kk == 0)
    def _():
        o_ref[...] = part

    @pl.when(kk > 0)
    def _():
        o_ref[...] += part

    @pl.when(kk == pl.num_programs(1) - 1)
    def _():
        x = x_ref[...] + g_ref[...] * o_ref[...]
        y = x * lax.rsqrt(jnp.mean(x * x, axis=-1, keepdims=True) + NORM_EPS)
        o_ref[...] = y * fw_ref[...]


def _ffn_down(act, w, x2, gate, fw, rows_per_mod):
    m, fh = act.shape
    d = w.shape[1]
    tm = _pick(rows_per_mod, 1024, 8)
    per = rows_per_mod // tm
    tk = _pick(fh, 512, LANES)
    return pl.pallas_call(
        _ffn_down_kernel, grid=(m // tm, fh // tk),
        in_specs=[pl.BlockSpec((tm, tk), lambda i, k: (i, k)),
                  pl.BlockSpec((tk, d), lambda i, k: (k, 0)),
                  pl.BlockSpec((tm, d), lambda i, k: (i, 0)),
                  pl.BlockSpec((None, 1, d), lambda i, k: (i // per, 0, 0)),
                  pl.BlockSpec((1, d), lambda i, k: (0, 0))],
        out_specs=pl.BlockSpec((tm, d), lambda i, k: (i, 0)),
        out_shape=jax.ShapeDtypeStruct((m, d), F32),
        compiler_params=_cp("parallel", "arbitrary", vmem=VMEM_LIMIT_BIG), name="ffn_down")(act, w, x2, gate, fw)


def _pad_cols(a, n):
    return jnp.pad(a, ((0, 0), (0, n - a.shape[1])))


def _roundup(n, m):
    return -(-n // m) * m


def kernel(x, c, ctx, c_ctx, w_ada, b_ada, norm1_w, w_in, rw_mu, rw_w0, rw_w2, rw_a0, rw_a2, rw_g2, rw_k_k, rw_k_a, rw_r_k, rw_ln_w, rw_ln_b, gdn_conv_w, gdn_a_log, gdn_dt_bias, gdn_norm_w, merge_p_a, merge_p_b, w_out, norm2_w, ffn_w_gate_up, ffn_w_down, final_norm_w):
    assert w_ada.shape[0] == 1, "single-layer trunk"
    bsz, t_len, d = x.shape
    tc_len = ctx.shape[1]
    h_rw, n_rw = rw_r_k.shape[1:]
    wr = h_rw * n_rw
    rk_w, rk_a, rk_g = rw_w2.shape[2], rw_a2.shape[2], rw_g2.shape[1]
    hg, dg = gdn_a_log.shape[-1], gdn_norm_w.shape[-1]
    wg = hg * dg
    assert n_rw == HALF and dg == LANES and h_rw % 2 == 0
    assert t_len % GRID_W == 0 and t_len % CHUNK == 0 and tc_len % CHUNK == 0

    w_in0 = w_in[0]
    o_wd = 3 * wr
    o_ad = o_wd + 2 * rk_w
    o_gd = o_ad + 2 * rk_a
    rw_cols = o_gd + rk_g
    wdp, adp = _roundup(2 * rk_w, LANES), _roundup(2 * rk_a, LANES)
    gdp = _roundup(o_wd + wdp + adp + rk_g, 2 * LANES) - (o_wd + wdp + adp)

    def rw_layout(a):
        return jnp.concatenate([a[:, :o_wd], _pad_cols(a[:, o_wd:o_ad], wdp), _pad_cols(a[:, o_ad:o_gd], adp),
                                _pad_cols(a[:, o_gd:rw_cols], gdp)], axis=1)

    o_cv = rw_cols
    o_z = o_cv + 3 * wg
    o_ab = o_z + wg
    o_gt = o_ab + 4 * hg
    w_rw = rw_layout(w_in0[:, :rw_cols]).astype(BF16)
    w_cv = w_in0[:, o_cv:o_z].astype(BF16)
    w_z = w_in0[:, o_z:o_ab].astype(BF16)
    w_ab = _pad_cols(w_in0[:, o_ab:o_gt], LANES).astype(BF16)
    w_gt = w_in0[:, o_gt:].astype(BF16)
    mu_p = rw_layout(rw_mu)

    w2f = jnp.zeros((wdp, 2 * wr), F32)
    a2f = jnp.zeros((adp, 2 * wr), F32)
    for dd in range(2):
        w2f = w2f.at[dd * rk_w:(dd + 1) * rk_w, dd * wr:(dd + 1) * wr].set(rw_w2[0, dd])
        a2f = a2f.at[dd * rk_a:(dd + 1) * rk_a, dd * wr:(dd + 1) * wr].set(rw_a2[0, dd])
    g2p = jnp.pad(rw_g2[0], ((0, gdp - rk_g), (0, 0)))
    li = jnp.arange(LANES)
    e64 = ((li[:, None] // HALF) == (li[None, :] // HALF)).astype(BF16)
    prm = dict(wr=wr, wdp=wdp, adp=adp,
               w0=rw_w0[0].reshape(1, 2 * wr), w2=w2f.astype(BF16),
               a0=rw_a0[0].reshape(1, 2 * wr), a2=a2f.astype(BF16), g2=g2p.astype(BF16),
               k_k=rw_k_k, k_a=rw_k_a, r_k=rw_r_k[0].reshape(1, wr), e64=e64)

    nega = _pad_cols((-jnp.exp(gdn_a_log[0])).reshape(1, 2 * hg), LANES)
    dtb = _pad_cols(gdn_dt_bias[0].reshape(1, 2 * hg), LANES)

    rows = _roundup(bsz + 1, 8)
    cc = jnp.concatenate([c, c_ctx[None, :], jnp.zeros((rows - bsz - 1, d), F32)], axis=0)
    mod = _ada(cc, w_ada[0], b_ada)
    mods = [mod[:bsz, i * d:(i + 1) * d].reshape(bsz, 1, d) for i in range(6)]
    sh1, sc1, gt1, sh2, sc2, gt2 = mods
    csh1 = mod[bsz:bsz + 1, 0:d].reshape(1, 1, d)
    csc1 = mod[bsz:bsz + 1, d:2 * d].reshape(1, 1, d)

    x2 = x.reshape(bsz * t_len, d)
    ctx2 = ctx.reshape(bsz * tc_len, d)
    h = _normmod(x2, norm1_w, sh1, sc1, t_len)
    hc = _normmod(ctx2, norm1_w, csh1, csc1, bsz * tc_len)

    def mixer_inputs(hh, seq_len, latent):
        p_rw = _matmul(hh, w_rw, tn=768, name="proj_rw")
        p_cv = _matmul(hh, w_cv, name="proj_cv")
        p_ab = _matmul(hh, w_ab, name="proj_ab")
        xs = _shiftmix(p_rw.reshape(bsz, seq_len, -1), mu_p, "grid" if latent else "seq")
        feat = _rwfeat(xs.reshape(bsz * seq_len, -1), prm, latent)
        qkv = _gconv(p_cv.reshape(bsz, seq_len, -1), gdn_conv_w[0], hg, dg ** -0.5)
        gb = _gfeat(p_ab, nega, dtb, hg)
        nc = seq_len // CHUNK
        grows = gb[:, :4 * hg].reshape(bsz, nc, CHUNK, 4, hg).transpose(0, 4, 1, 3, 2)
        return feat, qkv.reshape(bsz * seq_len, -1), grows

    feat_c, qkv_c, grows_c = mixer_inputs(hc, tc_len, False)
    feat_l, qkv_l, grows_l = mixer_inputs(h, t_len, True)

    def run_mixer(prep_c, prep_l, nseq):
        s0 = jnp.zeros((2, nseq, LANES, LANES), F32)
        (s_ctx,) = _scan(prep_c[:3], prep_c[3:], s0, False)
        o_f, o_r, _ = _scan(prep_l[:5], prep_l[5:], s_ctx, True)
        return o_f, o_r

    o_f, o_r = run_mixer(_rwprep(feat_c, bsz, tc_len, False), _rwprep(feat_l, bsz, t_len, True),
                         bsz * (wr // LANES))
    o_a = _rw_readout(o_f, o_r, feat_l[7], feat_l[6], rw_ln_w, rw_ln_b, e64, bsz)

    o_f, o_r = run_mixer(_gprep(qkv_c, grows_c, bsz, tc_len, hg, False),
                         _gprep(qkv_l, grows_l, bsz, t_len, hg, True), bsz * hg)
    z = _matmul(h, w_z, name="proj_z")
    o_b = _g_readout(o_f, o_r, z, gdn_norm_w, bsz)

    gt = _matmul(h, w_gt, name="proj_gt")
    mm = _merge(o_a, o_b, merge_p_a[0].astype(BF16), merge_p_b[0].astype(BF16), gt)
    x1 = _outproj(mm, w_out[0].astype(BF16), x2, gt1, t_len)
    h2 = _normmod(x1, norm2_w, sh2, sc2, t_len)
    act = _ffn_up(h2, ffn_w_gate_up[0].astype(BF16))
    out = _ffn_down(act, ffn_w_down[0].astype(BF16), x1, gt2, final_norm_w.reshape(1, d), t_len)
    return out.reshape(bsz, t_len, d)
```

```python
import functools
import math

import jax
import jax.numpy as jnp
from jax import lax
from jax.experimental import pallas as pl
from jax.experimental.pallas import tpu as pltpu

F32 = jnp.float32
BF16 = jnp.bfloat16

GRID_W = 64
NORM_EPS = 1e-6
RW_GN_EPS = 64e-5
CHUNK = 64
LANES = 128
HALF = 64
VMEM_LIMIT = 48 * 1024 * 1024
VMEM_LIMIT_BIG = 58 * 1024 * 1024


def _pick(n, pref, mult):
    t = min(pref, n)
    t -= t % mult
    while t > mult and n % t:
        t -= mult
    assert t >= mult and n % t == 0, (n, pref, mult)
    return t


def _cp(*sem, vmem=VMEM_LIMIT):
    return pltpu.CompilerParams(dimension_semantics=sem, vmem_limit_bytes=vmem)


def _sigmoid(x):
    return 1.0 / (1.0 + jnp.exp(-x))


def _silu(x):
    return x * _sigmoid(x)


def _softplus(x):
    return jnp.maximum(x, 0.0) + jnp.log(1.0 + jnp.exp(-jnp.abs(x)))


def _split(x):
    hi = x.astype(BF16)
    lo = (x - hi.astype(F32)).astype(BF16)
    return hi, lo


_NN = (((1,), (0,)), ((), ()))
_NT = (((1,), (1,)), ((), ()))
_TN = (((0,), (0,)), ((), ()))


def _dg(a, b, dn):
    return lax.dot_general(a, b, dn, preferred_element_type=F32)


def _mm(a, b, dn=_NN, passes=1):
    if passes == 1:
        return _dg(a.astype(BF16), b.astype(BF16), dn)
    ah, al = _split(a)
    bh, bl = _split(b)
    return _dg(ah, bh, dn) + (_dg(ah, bl, dn) + _dg(al, bh, dn))


def _mm_exact_lhs(a_bf, b, dn=_NN):
    bh, bl = _split(b)
    return _dg(a_bf, bh, dn) + _dg(a_bf, bl, dn)


def _iota(shape, dim):
    return lax.broadcasted_iota(jnp.int32, shape, dim)


def _ada_kernel(c_ref, w_ref, b_ref, o_ref):
    s = _silu(c_ref[...])
    o_ref[...] = _dg(s.astype(BF16), w_ref[...].astype(BF16), _NN) + b_ref[...]


def _ada(cc, w, b):
    m, d = cc.shape
    n = w.shape[1]
    tn = _pick(n, 1024, LANES)
    return pl.pallas_call(
        _ada_kernel, grid=(n // tn,),
        in_specs=[pl.BlockSpec((m, d), lambda j: (0, 0)),
                  pl.BlockSpec((d, tn), lambda j: (0, j)),
                  pl.BlockSpec((1, tn), lambda j: (0, j))],
        out_specs=pl.BlockSpec((m, tn), lambda j: (0, j)),
        out_shape=jax.ShapeDtypeStruct((m, n), F32),
        compiler_params=_cp("parallel"), name="ada")(cc, w, b)


def _normmod_rows(x, nw, sh, sc):
    y = x * lax.rsqrt(jnp.mean(x * x, axis=-1, keepdims=True) + NORM_EPS) * nw
    return (y * (1.0 + sc) + sh).astype(BF16)


def _inproj_kernel(x_ref, nw_ref, sh_ref, sc_ref, w_ref, o_ref, h_scr):
    @pl.when(pl.program_id(1) == 0)
    def _():
        h_scr[...] = _normmod_rows(x_ref[...], nw_ref[...], sh_ref[...], sc_ref[...])

    o_ref[...] = _dg(h_scr[...], w_ref[...], _NN).astype(o_ref.dtype)


def _inproj(x2, nw, sh, sc, w, ncols, tn, rows_per_mod):
    m, d = x2.shape
    tm = _pick(rows_per_mod, 1024, 8)
    per = rows_per_mod // tm
    return pl.pallas_call(
        _inproj_kernel, grid=(m // tm, ncols // tn),
        in_specs=[pl.BlockSpec((tm, d), lambda i, j: (i, 0)),
                  pl.BlockSpec((1, d), lambda i, j: (0, 0)),
                  pl.BlockSpec((None, 1, d), lambda i, j: (i // per, 0, 0)),
                  pl.BlockSpec((None, 1, d), lambda i, j: (i // per, 0, 0)),
                  pl.BlockSpec((d, tn), lambda i, j: (0, j))],
        out_specs=pl.BlockSpec((tm, tn), lambda i, j: (i, j)),
        out_shape=jax.ShapeDtypeStruct((m, ncols), BF16),
        scratch_shapes=[pltpu.VMEM((tm, d), BF16)],
        compiler_params=_cp("parallel", "arbitrary"), name="inproj")(x2, nw, sh, sc, w)


def _shiftmix(mode, t_len, i, p, prev, nxt_rows, mu):
    shp = p.shape
    tt = shp[0]
    t = i * tt + _iota(shp, 0)
    lane = _iota(shp, 1)
    if mode == "grid":
        cl = lane & 3
        w = t & (GRID_W - 1)
        left = jnp.where(w != 0, pltpu.roll(p, 1, 0), 0.0)
        right = jnp.where(w != GRID_W - 1, pltpu.roll(p, tt - 1, 0), 0.0)
        if tt > GRID_W:
            up = jnp.concatenate([prev, p[:tt - GRID_W]], axis=0)
            down = jnp.concatenate([p[GRID_W:], nxt_rows], axis=0)
        else:
            up, down = prev, nxt_rows
        up = jnp.where(t >= GRID_W, up, 0.0)
        down = jnp.where(t < t_len - GRID_W, down, 0.0)
        sh = jnp.where(cl == 0, left, jnp.where(cl == 1, right, jnp.where(cl == 2, up, down)))
    else:
        prv = jnp.where(t != 0, pltpu.roll(p, 1, 0), 0.0)
        nxt = jnp.where(t != t_len - 1, pltpu.roll(p, tt - 1, 0), 0.0)
        sh = jnp.where((lane & 1) == 0, prv, nxt)
    return p + (sh - p) * mu


def _segsum(x, e):
    outs = []
    for g in range(x.shape[1] // LANES):
        outs.append(_mm_exact_lhs_rhs(x[:, g * LANES:(g + 1) * LANES], e))
    return outs[0] if len(outs) == 1 else jnp.concatenate(outs, axis=1)


def _mm_exact_lhs_rhs(x, e_bf):
    hi, lo = _split(x)
    return _dg(hi, e_bf, _NN) + _dg(lo, e_bf, _NN)


def _rwfeat_kernel(wr, wdp, adp, latent, t_len, p_ref, prev_ref, next_ref, mu_ref, w0_ref, w2_ref, a0_ref, a2_ref,
                   g2_ref, kk_ref_, ka_ref, rk_ref, e_ref, *outs):
    if latent:
        r_o, v_o, kk_o, lw_o, kd_o, bd_o, gate_o, bonus_o = outs
    else:
        r_o, v_o, kk_o, lw_o, kd_o, bd_o = outs
    e = e_ref[...]
    xs = _shiftmix("grid" if latent else "seq", t_len, pl.program_id(1), p_ref[...].astype(F32),
                   prev_ref[...].astype(F32), next_ref[...].astype(F32), mu_ref[...])
    r = xs[:, 0:wr]
    k = xs[:, wr:2 * wr]
    v = xs[:, 2 * wr:3 * wr]
    wd = xs[:, 3 * wr:3 * wr + wdp]
    ad = xs[:, 3 * wr + wdp:3 * wr + wdp + adp]
    lw = _dg(jnp.tanh(wd).astype(BF16), w2_ref[...], _NN) + w0_ref[...]
    w_log = -_softplus(-lw) - 0.5
    logw = -jnp.exp(w_log)
    iclr = _sigmoid(_dg(ad.astype(BF16), a2_ref[...], _NN) + a0_ref[...])
    kk0 = k * kk_ref_[...]
    kk = kk0 * lax.rsqrt(_segsum(kk0 * kk0, e) + NORM_EPS)
    r_o[...] = r.astype(BF16)
    v_o[...] = v.astype(BF16)
    kk_o[...] = kk.astype(BF16)
    ka = ka_ref[...]
    kdirs = []
    for d in range(2):
        ic = iclr[:, d * wr:(d + 1) * wr]
        kd = k * (1.0 + (ic - 1.0) * ka)
        kdirs.append(kd)
        lw_o[d] = logw[:, d * wr:(d + 1) * wr]
        kd_o[d] = kd.astype(BF16)
        bd_o[d] = (kk * ic).astype(BF16)
    if latent:
        gd = xs[:, 3 * wr + wdp + adp:]
        gate_o[...] = _dg(_sigmoid(gd).astype(BF16), g2_ref[...], _NN).astype(BF16)
        kb = 0.5 * (kdirs[0] + kdirs[1])
        bonus_o[...] = (_segsum(r * kb * rk_ref[...], e) * v).astype(BF16)


def _rwfeat(p, prm, bsz, t_len, latent):
    m = bsz * t_len
    wr, wdp, adp, cpw = prm["wr"], prm["wdp"], prm["adp"], prm["cpw"]
    tm = _pick(t_len, 256, GRID_W) if latent else t_len
    nt = t_len // tm
    hb = tm // GRID_W
    nh = m // GRID_W
    full = lambda b, i: (0, 0)
    names = ("mu", "w0", "w2", "a0", "a2", "g2", "k_k", "k_a", "r_k", "e64")
    w_specs = [pl.BlockSpec(prm[n].shape, full) for n in names]
    tok = pl.BlockSpec((tm, wr), lambda b, i: (b * nt + i, 0))
    tok2 = pl.BlockSpec((2, tm, wr), lambda b, i: (0, b * nt + i, 0))
    s1 = jax.ShapeDtypeStruct((m, wr), BF16)
    s2 = jax.ShapeDtypeStruct((2, m, wr), BF16)
    out_specs = [tok, tok, tok, tok2, tok2, tok2]
    out_shape = [s1, s1, s1, jax.ShapeDtypeStruct((2, m, wr), F32), s2, s2]
    if latent:
        out_specs += [tok, tok]
        out_shape += [s1, s1]
    kern = functools.partial(_rwfeat_kernel, wr, wdp, adp, latent, t_len)
    prev = pl.BlockSpec((GRID_W, cpw), lambda b, i: (jnp.maximum((b * nt + i) * hb - 1, 0), 0))
    nxt = pl.BlockSpec((GRID_W, cpw), lambda b, i: (jnp.minimum((b * nt + i + 1) * hb, nh - 1), 0))
    return pl.pallas_call(
        kern, grid=(bsz, nt),
        in_specs=[pl.BlockSpec((tm, cpw), lambda b, i: (b * nt + i, 0)), prev, nxt] + w_specs,
        out_specs=out_specs, out_shape=out_shape,
        compiler_params=_cp("parallel", "parallel"), name="rwfeat")(p, p, p, *[prm[n] for n in names])


INV_PASSES = 1
PREP_PASSES = 1
PREP_UNROLL = 8


def _prep_unroll(tc):
    u = PREP_UNROLL
    while tc % u:
        u //= 2
    return u


def _stack(x, h0):
    return jnp.concatenate([jnp.where(h0, x, 0.0), jnp.where(h0, 0.0, x)], axis=0)


def _tri_inverse_many(mats, eye2, h0):
    c = CHUNK
    steps = int(math.log2(c)) - 1
    ps = [eye2 - a for a in mats]
    xs = [_mm(a, _stack(a, h0), passes=INV_PASSES) for a in mats]
    for i in range(steps):
        last = i == steps - 1
        for n in range(len(mats)):
            rhs = _stack(xs[n], h0)
            if last:
                ps[n] = ps[n] + _mm(ps[n], rhs, passes=INV_PASSES)
            else:
                both = _mm(jnp.concatenate([ps[n], xs[n]], axis=0), rhs, passes=INV_PASSES)
                ps[n] = ps[n] + both[:c]
                xs[n] = both[c:]
    return ps


def _rwprep_chunks(need_out, insts):
    c = CHUNK
    shp = (c, LANES)
    row = _iota(shp, 0)
    lane = _iota(shp, 1)
    col = lane & (HALF - 1)
    h0 = lane < HALF
    eye2 = jnp.where(col == row, 1.0, 0.0)
    bdm = (_iota((LANES, LANES), 0) < HALF) == (_iota((LANES, LANES), 1) < HALF)
    cc_r, cc_c = _iota((c, c), 0), _iota((c, c), 1)
    tri_bf = {False: jnp.where(cc_c <= cc_r, 1.0, 0.0).astype(BF16),
              True: jnp.where(cc_c >= cc_r, 1.0, 0.0).astype(BF16)}
    strict = {False: col < row, True: col > row}
    incl = {False: col <= row, True: col >= row}
    n = len(insts)
    gs = [_mm_exact_lhs(tri_bf[rev], lw) for (rev, r, v, kk, lw, k, bb) in insts]
    st = []
    for (rev, r, v, kk, lw, k, bb), g in zip(insts, gs):
        last, mid = (0, c // 2) if rev else (c - 1, c // 2 - 1)
        gm = g - lw
        gtot = g[last:last + 1, :]
        rho = g[mid:mid + 1, :]
        e_inv = jnp.exp(rho - g)
        kkd = kk * jnp.exp(gm - rho)
        et = jnp.exp(gtot - g)
        d = dict(rev=rev, r=r, v=v, g=g, kt=k * et, bbt=bb * et, pc=jnp.exp(gtot), kkd0=kk * jnp.exp(gm))
        lhs = jnp.concatenate([kkd, r * jnp.exp(g - rho)], axis=0) if need_out else kkd
        rhs = jnp.concatenate([_stack(bb * e_inv, h0), _stack(k * e_inv, h0)], axis=0)
        d["ab"] = _mm(lhs, rhs, _NT, PREP_PASSES)
        st.append(d)
    for d in st:
        ab = d["ab"]
        d["a"] = jnp.where(strict[d["rev"]], ab[:c, :LANES], 0.0)
        d["sv"] = _stack(d["v"], h0)
        d["bv"] = _mm(jnp.where(strict[d["rev"]], ab[:c, LANES:], 0.0), d["sv"], passes=PREP_PASSES)
    ts = _tri_inverse_many([d["a"] for d in st], eye2, h0)
    for d, t in zip(st, ts):
        wu = _mm(t, jnp.concatenate([_stack(d["kkd0"], h0), _stack(d["bv"], h0)], axis=1), passes=PREP_PASSES)
        d["w"], d["u0"] = wu[:, :LANES], wu[:, LANES:]
    out = []
    for d in st:
        w, u0, bbt = d["w"], d["u0"], d["bbt"]
        mn = jnp.where(bdm, _mm(bbt, w, _TN, PREP_PASSES), 0.0)
        nnt = jnp.where(bdm, _mm(d["v"], d["kt"], _TN, PREP_PASSES) - _mm(u0, bbt, _TN, PREP_PASSES), 0.0)
        if not need_out:
            out.append((None, None, mn, nnt, d["pc"]))
            continue
        ab = d["ab"]
        ar = jnp.where(incl[d["rev"]], ab[c:, :LANES], 0.0)
        br = jnp.where(incl[d["rev"]], ab[c:, LANES:], 0.0)
        arwu = _mm(ar, jnp.concatenate([_stack(w, h0), _stack(u0, h0)], axis=1), passes=PREP_PASSES)
        qp = d["r"] * jnp.exp(d["g"]) - arwu[:, :LANES]
        o0 = _mm(br, d["sv"], passes=PREP_PASSES) - arwu[:, LANES:]
        out.append((qp, o0, mn, nnt, d["pc"]))
    return out


def _rwprep_kernel(tc, need_out, r_ref, v_ref, kk_ref, lwf_ref, lwr_ref, kf_ref, kr_ref, bf_ref, br_ref, *outs):
    per = 5 if need_out else 3
    outs_d = (outs[:per], outs[per:])
    ins_d = ((lwf_ref, kf_ref, bf_ref), (lwr_ref, kr_ref, br_ref))

    unroll = _prep_unroll(tc)

    def body(it, carry):
        insts, where = [], []
        for u in range(unroll):
            ci = it * unroll + u
            sl = pl.ds(pl.multiple_of(ci * CHUNK, CHUNK), CHUNK)
            r, v, kk = (ref[sl, :].astype(F32) for ref in (r_ref, v_ref, kk_ref))
            for d in range(2):
                lw_ref, k_ref, b_ref = ins_d[d]
                insts.append((d == 1, r, v, kk, lw_ref[sl, :], k_ref[sl, :].astype(F32),
                              b_ref[sl, :].astype(F32)))
                where.append((d, ci, sl))
        for (d, ci, sl), (qp, o0, mn, nnt, pc) in zip(where, _rwprep_chunks(need_out, insts)):
            o = outs_d[d]
            if need_out:
                o[0][sl, :] = qp.astype(BF16)
                o[1][sl, :] = o0.astype(BF16)
                o = o[2:]
            o[0][ci] = mn.astype(BF16)
            o[1][ci] = nnt.astype(BF16)
            o[2][pl.ds(ci, 1), :] = pc
        return carry

    lax.fori_loop(0, tc // unroll, body, 0)


def _prep_out(nseq, t_len, need_out):
    nc = t_len // CHUNK
    shapes = []
    if need_out:
        shapes += [jax.ShapeDtypeStruct((nseq, t_len, LANES), BF16)] * 2
    shapes += [jax.ShapeDtypeStruct((nseq, nc, LANES, LANES), BF16)] * 2
    shapes += [jax.ShapeDtypeStruct((nseq, nc, LANES), F32)]
    return shapes


def _prep_out_specs(tc, need_out, seq_of):
    specs = []
    if need_out:
        specs += [pl.BlockSpec((None, tc * CHUNK, LANES), lambda b, p, j: (seq_of(b, p), j, 0))] * 2
    specs += [pl.BlockSpec((None, tc, LANES, LANES), lambda b, p, j: (seq_of(b, p), j, 0, 0))] * 2
    specs += [pl.BlockSpec((None, tc, LANES), lambda b, p, j: (seq_of(b, p), j, 0))]
    return specs


def _chunks_per_step(nc):
    return 8 if nc % 8 == 0 else nc


def _rwprep(feat, bsz, t_len, need_out):
    r, v, kk, lw, kd, bd = feat[:6]
    wr = r.shape[1]
    npair = wr // LANES
    nc = t_len // CHUNK
    tc = _chunks_per_step(nc)
    nblk = nc // tc
    tok = pl.BlockSpec((tc * CHUNK, LANES), lambda b, p, j: (b * nblk + j, p))
    tok_d = lambda d: pl.BlockSpec((None, tc * CHUNK, LANES), lambda b, p, j: (d, b * nblk + j, p))
    seq_of = lambda b, p: b * npair + p
    kern = functools.partial(_rwprep_kernel, tc, need_out)
    return pl.pallas_call(
        kern, grid=(bsz, npair, nblk),
        in_specs=[tok, tok, tok, tok_d(0), tok_d(1), tok_d(0), tok_d(1), tok_d(0), tok_d(1)],
        out_specs=_prep_out_specs(tc, need_out, seq_of) * 2,
        out_shape=_prep_out(bsz * npair, t_len, need_out) * 2,
        compiler_params=_cp("parallel", "parallel", "parallel"), name="rwprep")(
            r, v, kk, lw, lw, kd, kd, bd, bd)


SCAN_GROUP = 8
SCAN_CHUNKS = 8


def _scan_kernel(tc, ng, need_out, *refs):
    per = 5 if need_out else 3
    ins = (refs[:per], refs[per:2 * per])
    s0_ref = refs[2 * per]
    rest = refs[2 * per + 1:]
    if need_out:
        o_refs = rest[:2]
        rest = rest[2:]
    sfin_ref, s_scr = rest
    j = pl.program_id(1)

    @pl.when(j == 0)
    def _():
        s_scr[...] = s0_ref[...]

    def body(i, carry):
        for g in range(ng):
            for d in range(2):
                s = s_scr[d, g]
                s_bf = s.astype(BF16)
                ci = i if d == 0 else tc - 1 - i
                refs_d = ins[d]
                if need_out:
                    sl = pl.ds(pl.multiple_of(ci * CHUNK, CHUNK), CHUNK)
                    o = _dg(refs_d[0][g, sl, :], s_bf, _NT) + refs_d[1][g, sl, :].astype(F32)
                    o_refs[d][g, sl, :] = o.astype(o_refs[d].dtype)
                    refs_d = refs_d[2:]
                upd = _dg(s_bf, refs_d[0][g, ci], _NT)
                s_scr[d, g] = s * refs_d[2][g, ci] - upd + refs_d[1][g, ci].astype(F32)
        return carry

    lax.fori_loop(0, tc, body, 0)

    @pl.when(j == pl.num_programs(1) - 1)
    def _():
        sfin_ref[...] = s_scr[...]


def _scan(prep_f, prep_r, s0, need_out):
    mn = prep_f[-3]
    nseq, nc = mn.shape[0], mn.shape[1]
    tc = _pick(nc, SCAN_CHUNKS, 1)
    nblk = nc // tc
    ng = _pick(nseq, SCAN_GROUP, 1)
    prep_f = list(prep_f[:-1]) + [prep_f[-1].reshape(nseq, nc, 1, LANES)]
    prep_r = list(prep_r[:-1]) + [prep_r[-1].reshape(nseq, nc, 1, LANES)]

    def specs(rev):
        jj = (lambda j: nblk - 1 - j) if rev else (lambda j: j)
        sp = []
        if need_out:
            sp += [pl.BlockSpec((ng, tc * CHUNK, LANES), lambda s, j: (s, jj(j), 0))] * 2
        sp += [pl.BlockSpec((ng, tc, LANES, LANES), lambda s, j: (s, jj(j), 0, 0))] * 2
        sp += [pl.BlockSpec((ng, tc, 1, LANES), lambda s, j: (s, jj(j), 0, 0))]
        return sp

    st_spec = pl.BlockSpec((2, ng, LANES, LANES), lambda s, j: (0, s, 0, 0))
    out_specs, out_shape = [], []
    if need_out:
        out_specs += [specs(False)[0], specs(True)[0]]
        out_shape += [jax.ShapeDtypeStruct((nseq, nc * CHUNK, LANES), BF16)] * 2
    out_specs.append(st_spec)
    out_shape.append(jax.ShapeDtypeStruct((2, nseq, LANES, LANES), F32))
    kern = functools.partial(_scan_kernel, tc, ng, need_out)
    return pl.pallas_call(
        kern, grid=(nseq // ng, nblk),
        in_specs=specs(False) + specs(True) + [st_spec],
        out_specs=out_specs, out_shape=out_shape,
        scratch_shapes=[pltpu.VMEM((2, ng, LANES, LANES), F32)],
        compiler_params=_cp("parallel", "arbitrary"), name="scan")(*prep_f, *prep_r, s0)


READOUT_GROUP = 4


def _rw_readout_kernel(of_ref, or_ref, bonus_ref, gate_ref, lnw_ref, lnb_ref, e_ref, o_ref):
    e = e_ref[...]
    inv = 1.0 / HALF
    for g in range(of_ref.shape[0]):
        cs = slice(g * LANES, (g + 1) * LANES)
        y = of_ref[g].astype(F32) + or_ref[g].astype(F32)
        mu = _segsum(y, e) * inv
        yc = y - mu
        var = _segsum(yc * yc, e) * inv
        yn = yc * lax.rsqrt(var + RW_GN_EPS) * lnw_ref[:, cs] + lnb_ref[:, cs]
        o_ref[:, cs] = ((yn + bonus_ref[:, cs].astype(F32)) * gate_ref[:, cs].astype(F32)).astype(o_ref.dtype)


def _rw_readout(o_f, o_r, bonus, gate, lnw, lnb, e64, bsz):
    nseq, t_len, _ = o_f.shape
    npair = nseq // bsz
    ng = _pick(npair, READOUT_GROUP, 1)
    npg = npair // ng
    tt = _pick(t_len, 512, 8)
    nt = t_len // tt
    seq = pl.BlockSpec((ng, tt, LANES), lambda b, p, i: (b * npg + p, i, 0))
    tok = pl.BlockSpec((tt, ng * LANES), lambda b, p, i: (b * nt + i, p))
    par = pl.BlockSpec((1, ng * LANES), lambda b, p, i: (0, p))
    return pl.pallas_call(
        _rw_readout_kernel, grid=(bsz, npg, nt),
        in_specs=[seq, seq, tok, tok, par, par, pl.BlockSpec((LANES, LANES), lambda b, p, i: (0, 0))],
        out_specs=tok, out_shape=jax.ShapeDtypeStruct(bonus.shape, BF16),
        compiler_params=_cp("parallel", "parallel", "parallel"), name="rw_readout")(
            o_f, o_r, bonus, gate, lnw, lnb, e64)


CONV_HALO = 16


def _gconv_kernel(taps, nq, scale, p_ref, prev_ref, next_ref, w_ref, o_ref):
    i = pl.program_id(1)
    cb = pl.program_id(2)
    main = p_ref[...].astype(F32)
    tt = main.shape[0]
    prev = jnp.where(i > 0, prev_ref[...].astype(F32), 0.0)
    nxt = jnp.where(i < pl.num_programs(1) - 1, next_ref[...].astype(F32), 0.0)
    xcat = jnp.concatenate([prev, main, nxt], axis=0)
    n = tt + 2 * CONV_HALO
    half = taps // 2
    acc = None
    for j in range(taps):
        sh = (half - j) % n
        xs = xcat if sh == 0 else pltpu.roll(xcat, sh, 0)
        term = xs[CONV_HALO:CONV_HALO + tt] * w_ref[j:j + 1, :]
        acc = term if acc is None else acc + term
    u = _silu(acc)
    heads = u.shape[1] // LANES
    for s in range(heads):
        hd = cb * heads + s
        us = u[:, s * LANES:(s + 1) * LANES]
        un = us * lax.rsqrt(jnp.sum(us * us, axis=-1, keepdims=True) + NORM_EPS)
        un = un * jnp.where(hd < nq, scale, 1.0)
        o_ref[:, s * LANES:(s + 1) * LANES] = jnp.where(hd < 2 * nq, un, us).astype(o_ref.dtype)


def _gconv(p, col0, c3, cb, conv_w, bsz, t_len, nq, scale):
    m = bsz * t_len
    taps = conv_w.shape[0]
    assert taps // 2 <= CONV_HALO and col0 % cb == 0 and c3 % cb == 0
    tt = _pick(t_len, 512, CONV_HALO)
    nt = t_len // tt
    hb = tt // CONV_HALO
    nh = m // CONV_HALO
    c0 = col0 // cb
    kern = functools.partial(_gconv_kernel, taps, nq, scale)
    return pl.pallas_call(
        kern, grid=(bsz, nt, c3 // cb),
        in_specs=[pl.BlockSpec((tt, cb), lambda b, i, c: (b * nt + i, c0 + c)),
                  pl.BlockSpec((CONV_HALO, cb), lambda b, i, c: (jnp.maximum((b * nt + i) * hb - 1, 0), c0 + c)),
                  pl.BlockSpec((CONV_HALO, cb),
                               lambda b, i, c: (jnp.minimum((b * nt + i + 1) * hb, nh - 1), c0 + c)),
                  pl.BlockSpec((taps, cb), lambda b, i, c: (0, c))],
        out_specs=pl.BlockSpec((tt, cb), lambda b, i, c: (b * nt + i, c)),
        out_shape=jax.ShapeDtypeStruct((m, c3), BF16),
        compiler_params=_cp("parallel", "parallel", "parallel"), name="gconv")(p, p, p, conv_w)


def _gfeat_kernel(hg, ab_ref, nega_ref, dtb_ref, o_ref):
    ab = ab_ref[...].astype(F32)
    tm = ab.shape[0]
    lane = _iota(ab.shape, 1)
    g = jnp.where(lane < 2 * hg, nega_ref[...] * _softplus(ab + dtb_ref[...]), 0.0)
    beta = _sigmoid(ab)
    rr = _iota((tm, tm), 0)
    cc = _iota((tm, tm), 1)
    same = (rr // CHUNK) == (cc // CHUNK)
    lf = jnp.where(same & (cc <= rr), 1.0, 0.0).astype(BF16)
    lr = jnp.where(same & (cc >= rr), 1.0, 0.0).astype(BF16)
    gcf = _mm_exact_lhs(lf, g)
    gcr = _mm_exact_lhs(lr, g)
    gc = jnp.where(lane < hg, gcf, gcr)
    o_ref[...] = jnp.where(lane < 2 * hg, gc, beta)


def _gfeat(p, col0, nega, dtb, hg):
    m = p.shape[0]
    tm = _pick(m, 256, CHUNK)
    c0 = col0 // LANES
    return pl.pallas_call(
        functools.partial(_gfeat_kernel, hg), grid=(m // tm,),
        in_specs=[pl.BlockSpec((tm, LANES), lambda i: (i, c0)),
                  pl.BlockSpec((1, LANES), lambda i: (0, 0)),
                  pl.BlockSpec((1, LANES), lambda i: (0, 0))],
        out_specs=pl.BlockSpec((tm, LANES), lambda i: (i, 0)),
        out_shape=jax.ShapeDtypeStruct((m, LANES), F32),
        compiler_params=_cp("parallel"), name="gfeat")(p, nega, dtb)


def _col_from_row(rowv, eye):
    c = rowv.shape[1]
    return jnp.sum(jnp.where(eye, jnp.broadcast_to(rowv, (c, c)), 0.0), axis=1, keepdims=True)


def _gprep_chunks(need_out, insts):
    c = CHUNK
    shp = (c, LANES)
    row = _iota(shp, 0)
    lane = _iota(shp, 1)
    col = lane & (HALF - 1)
    f0 = lane < HALF
    eye = _iota((c, c), 0) == _iota((c, c), 1)
    ahead = jnp.where(f0, row - col, col - row)
    strict = ahead > 0
    incl = ahead >= 0
    eye2 = jnp.where(col == row, 1.0, 0.0)
    st = []
    for (q, k, v, rows) in insts:
        kk2 = jnp.concatenate([k, k], axis=0)
        lhs = jnp.concatenate([k, q], axis=0) if need_out else k
        st.append(dict(q=q, k=k, v=v, rows=rows, kq=_mm(lhs, kk2, _NT, PREP_PASSES)))
    for d in st:
        rows = d["rows"]
        gr = (rows[0:1, :], rows[1:2, :])
        gcol = [_col_from_row(x, eye) for x in gr]
        bcol = [_col_from_row(rows[2 + dd:3 + dd, :], eye) for dd in range(2)]
        grow2 = jnp.concatenate(gr, axis=1)
        gcol2 = jnp.where(f0, gcol[0], gcol[1])
        bcol2 = jnp.where(f0, bcol[0], bcol[1])
        decay = jnp.where(incl, jnp.exp(jnp.where(incl, gcol2 - grow2, 0.0)), 0.0)
        d.update(gcol=gcol, bcol=bcol, decay=decay, glast=(gr[0][:, c - 1:c], gr[1][:, 0:1]),
                 lmat=jnp.where(strict, d["kq"][:c] * bcol2 * decay, 0.0))
    ts = _tri_inverse_many([d["lmat"] for d in st], eye2, f0)
    for d, t in zip(st, ts):
        k, v, gcol, bcol = d["k"], d["v"], d["gcol"], d["bcol"]
        egc = [jnp.exp(gcol[dd]) for dd in range(2)]
        rhs2 = jnp.concatenate([jnp.concatenate([v * bcol[dd], k * (bcol[dd] * egc[dd])], axis=1)
                                for dd in range(2)], axis=0)
        d["egc"] = egc
        d["sol"] = [_mm(jnp.where(f0, t, 0.0), rhs2, passes=PREP_PASSES),
                    _mm(jnp.where(f0, 0.0, t), rhs2, passes=PREP_PASSES)]
    res = []
    for d in st:
        sol, k, q = d["sol"], d["k"], d["q"]
        if need_out:
            qk = jnp.where(incl, d["kq"][c:] * d["decay"], 0.0)
            sol2 = jnp.concatenate(sol, axis=0)
            xx = [_mm(jnp.where(f0, qk, 0.0), sol2, passes=PREP_PASSES),
                  _mm(jnp.where(f0, 0.0, qk), sol2, passes=PREP_PASSES)]
        out = []
        for dd in range(2):
            u = sol[dd][:, :LANES]
            wk = sol[dd][:, LANES:]
            ktail = k * jnp.exp(d["glast"][dd] - d["gcol"][dd])
            mn = _mm(ktail, wk, _TN, PREP_PASSES)
            nnt = _mm(u, ktail, _TN, PREP_PASSES)
            pc = jnp.broadcast_to(jnp.exp(d["glast"][dd]), (1, LANES))
            if need_out:
                out.append((q * d["egc"][dd] - xx[dd][:, LANES:], xx[dd][:, :LANES], mn, nnt, pc))
            else:
                out.append((None, None, mn, nnt, pc))
        res.append(out)
    return res


def _gprep_kernel(tc, need_out, q_ref, k_ref, v_ref, rows_ref, *outs):
    per = 5 if need_out else 3
    outs_d = (outs[:per], outs[per:])

    unroll = _prep_unroll(tc)

    def body(it, carry):
        insts, where = [], []
        for u in range(unroll):
            ci = it * unroll + u
            sl = pl.ds(pl.multiple_of(ci * CHUNK, CHUNK), CHUNK)
            q = q_ref[sl, :].astype(F32) if need_out else None
            insts.append((q, k_ref[sl, :].astype(F32), v_ref[sl, :].astype(F32), rows_ref[ci]))
            where.append((ci, sl))
        for (ci, sl), res in zip(where, _gprep_chunks(need_out, insts)):
            for d in range(2):
                qp, o0, mn, nnt, pc = res[d]
                o = outs_d[d]
                if need_out:
                    o[0][sl, :] = qp.astype(BF16)
                    o[1][sl, :] = o0.astype(BF16)
                    o = o[2:]
                o[0][ci] = mn.astype(BF16)
                o[1][ci] = nnt.astype(BF16)
                o[2][pl.ds(ci, 1), :] = pc
        return carry

    lax.fori_loop(0, tc // unroll, body, 0)


def _gprep(qkv, rows, bsz, t_len, hg, need_out):
    nc = t_len // CHUNK
    tc = _chunks_per_step(nc)
    nblk = nc // tc
    tok = lambda off: pl.BlockSpec((tc * CHUNK, LANES), lambda b, h, j: (b * nblk + j, off + h))
    seq_of = lambda b, h: b * hg + h
    kern = functools.partial(_gprep_kernel, tc, need_out)
    return pl.pallas_call(
        kern, grid=(bsz, hg, nblk),
        in_specs=[tok(0), tok(hg), tok(2 * hg),
                  pl.BlockSpec((None, None, tc, 4, CHUNK), lambda b, h, j: (b, h, j, 0, 0))],
        out_specs=_prep_out_specs(tc, need_out, seq_of) * 2,
        out_shape=_prep_out(bsz * hg, t_len, need_out) * 2,
        compiler_params=_cp("parallel", "parallel", "parallel"), name="gprep")(qkv, qkv, qkv, rows)


def _g_readout_kernel(of_ref, or_ref, z_ref, nw_ref, o_ref):
    for g in range(of_ref.shape[0]):
        cs = slice(g * LANES, (g + 1) * LANES)
        o = of_ref[g].astype(F32) + or_ref[g].astype(F32)
        on = o * lax.rsqrt(jnp.mean(o * o, axis=-1, keepdims=True) + NORM_EPS) * nw_ref[...]
        o_ref[:, cs] = (on * _silu(z_ref[:, cs].astype(F32))).astype(o_ref.dtype)


def _g_readout(o_f, o_r, p, col0, nw, bsz):
    nseq, t_len, _ = o_f.shape
    hg = nseq // bsz
    ng = _pick(hg, READOUT_GROUP, 1)
    npg = hg // ng
    tt = _pick(t_len, 512, 8)
    nt = t_len // tt
    assert col0 % (ng * LANES) == 0
    c0 = col0 // (ng * LANES)
    seq = pl.BlockSpec((ng, tt, LANES), lambda b, h, i: (b * npg + h, i, 0))
    ztok = pl.BlockSpec((tt, ng * LANES), lambda b, h, i: (b * nt + i, c0 + h))
    tok = pl.BlockSpec((tt, ng * LANES), lambda b, h, i: (b * nt + i, h))
    return pl.pallas_call(
        _g_readout_kernel, grid=(bsz, npg, nt),
        in_specs=[seq, seq, ztok, pl.BlockSpec((1, LANES), lambda b, h, i: (0, 0))],
        out_specs=tok, out_shape=jax.ShapeDtypeStruct((bsz * t_len, hg * LANES), BF16),
        compiler_params=_cp("parallel", "parallel", "parallel"), name="g_readout")(o_f, o_r, p, nw)


def _merge_kernel(oa_ref, ob_ref, pa_ref, pb_ref, ga_ref, gb_ref, o_ref):
    a = _dg(oa_ref[...], pa_ref[...], _NN)
    b = _dg(ob_ref[...], pb_ref[...], _NN)
    ga = _sigmoid(ga_ref[...].astype(F32))
    gb = _sigmoid(gb_ref[...].astype(F32))
    o_ref[...] = (ga * a + gb * b).astype(o_ref.dtype)


def _merge(oa, ob, pa, pb, p, col0, tn):
    m, wa = oa.shape
    wb = ob.shape[1]
    d = pa.shape[1]
    tm = _pick(m, 1024, 8)
    assert d % tn == 0 and col0 % tn == 0
    nj = d // tn
    c0 = col0 // tn
    return pl.pallas_call(
        _merge_kernel, grid=(m // tm, nj),
        in_specs=[pl.BlockSpec((tm, wa), lambda i, j: (i, 0)),
                  pl.BlockSpec((tm, wb), lambda i, j: (i, 0)),
                  pl.BlockSpec((wa, tn), lambda i, j: (0, j)),
                  pl.BlockSpec((wb, tn), lambda i, j: (0, j)),
                  pl.BlockSpec((tm, tn), lambda i, j: (i, c0 + j)),
                  pl.BlockSpec((tm, tn), lambda i, j: (i, c0 + nj + j))],
        out_specs=pl.BlockSpec((tm, tn), lambda i, j: (i, j)),
        out_shape=jax.ShapeDtypeStruct((m, d), BF16),
        compiler_params=_cp("parallel", "parallel"), name="merge")(oa, ob, pa, pb, p, p)


def _outproj_kernel(m_ref, w_ref, x_ref, g_ref, o_ref):
    o_ref[...] = x_ref[...] + g_ref[...] * _dg(m_ref[...], w_ref[...], _NN)


def _outproj(mm, w, x2, gate, rows_per_mod):
    m, k = mm.shape
    d = w.shape[1]
    tm = _pick(rows_per_mod, 1024, 8)
    per = rows_per_mod // tm
    tn = _pick(d, 512, LANES)
    return pl.pallas_call(
        _outproj_kernel, grid=(m // tm, d // tn),
        in_specs=[pl.BlockSpec((tm, k), lambda i, j: (i, 0)),
                  pl.BlockSpec((k, tn), lambda i, j: (0, j)),
                  pl.BlockSpec((tm, tn), lambda i, j: (i, j)),
                  pl.BlockSpec((None, 1, tn), lambda i, j: (i // per, 0, j))],
        out_specs=pl.BlockSpec((tm, tn), lambda i, j: (i, j)),
        out_shape=jax.ShapeDtypeStruct((m, d), F32),
        compiler_params=_cp("parallel", "parallel"), name="outproj")(mm, w, x2, gate)


def _ffn_up_kernel(x_ref, nw_ref, sh_ref, sc_ref, wg_ref, wu_ref, o_ref, h_scr):
    @pl.when(pl.program_id(1) == 0)
    def _():
        h_scr[...] = _normmod_rows(x_ref[...], nw_ref[...], sh_ref[...], sc_ref[...])

    h = h_scr[...]
    g = _dg(h, wg_ref[...], _NN)
    u = _dg(h, wu_ref[...], _NN)
    o_ref[...] = (_silu(g) * u).astype(o_ref.dtype)


def _ffn_up(x2, nw, sh, sc, wgu, rows_per_mod):
    m, d = x2.shape
    fh = wgu.shape[1] // 2
    tm = _pick(rows_per_mod, 1024, 8)
    per = rows_per_mod // tm
    tn = _pick(fh, 512, LANES)
    nj = fh // tn
    return pl.pallas_call(
        _ffn_up_kernel, grid=(m // tm, nj),
        in_specs=[pl.BlockSpec((tm, d), lambda i, j: (i, 0)),
                  pl.BlockSpec((1, d), lambda i, j: (0, 0)),
                  pl.BlockSpec((None, 1, d), lambda i, j: (i // per, 0, 0)),
                  pl.BlockSpec((None, 1, d), lambda i, j: (i // per, 0, 0)),
                  pl.BlockSpec((d, tn), lambda i, j: (0, j)),
                  pl.BlockSpec((d, tn), lambda i, j: (0, j + nj))],
        out_specs=pl.BlockSpec((tm, tn), lambda i, j: (i, j)),
        out_shape=jax.ShapeDtypeStruct((m, fh), BF16),
        scratch_shapes=[pltpu.VMEM((tm, d), BF16)],
        compiler_params=_cp("parallel", "arbitrary"), name="ffn_up")(x2, nw, sh, sc, wgu, wgu)


def _ffn_down_kernel(a_ref, w_ref, x_ref, g_ref, fw_ref, o_ref):
    kk = pl.program_id(1)

    @pl.when(kk == 0)
    def _():
        o_ref[...] = jnp.zeros_like(o_ref)

    o_ref[...] += _dg(a_ref[...], w_ref[...], _NN)

    @pl.when(kk == pl.num_programs(1) - 1)
    def _():
        x = x_ref[...] + g_ref[...] * o_ref[...]
        y = x * lax.rsqrt(jnp.mean(x * x, axis=-1, keepdims=True) + NORM_EPS)
        o_ref[...] = y * fw_ref[...]


def _ffn_down(act, w, x2, gate, fw, rows_per_mod):
    m, fh = act.shape
    d = w.shape[1]
    tm = _pick(rows_per_mod, 1024, 8)
    per = rows_per_mod // tm
    tk = _pick(fh, 512, LANES)
    return pl.pallas_call(
        _ffn_down_kernel, grid=(m // tm, fh // tk),
        in_specs=[pl.BlockSpec((tm, tk), lambda i, k: (i, k)),
                  pl.BlockSpec((tk, d), lambda i, k: (k, 0)),
                  pl.BlockSpec((tm, d), lambda i, k: (i, 0)),
                  pl.BlockSpec((None, 1, d), lambda i, k: (i // per, 0, 0)),
                  pl.BlockSpec((1, d), lambda i, k: (0, 0))],
        out_specs=pl.BlockSpec((tm, d), lambda i, k: (i, 0)),
        out_shape=jax.ShapeDtypeStruct((m, d), F32),
        compiler_params=_cp("parallel", "arbitrary", vmem=VMEM_LIMIT_BIG), name="ffn_down")(act, w, x2, gate, fw)


def _pad_cols(a, n):
    return jnp.pad(a, ((0, 0), (0, n - a.shape[1])))


def _roundup(n, m):
    return -(-n // m) * m


def kernel(x, c, ctx, c_ctx, w_ada, b_ada, norm1_w, w_in, rw_mu, rw_w0, rw_w2, rw_a0, rw_a2, rw_g2, rw_k_k, rw_k_a, rw_r_k, rw_ln_w, rw_ln_b, gdn_conv_w, gdn_a_log, gdn_dt_bias, gdn_norm_w, merge_p_a, merge_p_b, w_out, norm2_w, ffn_w_gate_up, ffn_w_down, final_norm_w):
    assert w_ada.shape[0] == 1, "single-layer trunk"
    bsz, t_len, d = x.shape
    tc_len = ctx.shape[1]
    h_rw, n_rw = rw_r_k.shape[1:]
    wr = h_rw * n_rw
    rk_w, rk_a, rk_g = rw_w2.shape[2], rw_a2.shape[2], rw_g2.shape[1]
    hg, dg = gdn_a_log.shape[-1], gdn_norm_w.shape[-1]
    wg = hg * dg
    assert n_rw == HALF and dg == LANES and h_rw % 2 == 0
    assert t_len % GRID_W == 0 and t_len % CHUNK == 0 and tc_len % CHUNK == 0

    w_in0 = w_in[0]
    o_wd = 3 * wr
    o_ad = o_wd + 2 * rk_w
    o_gd = o_ad + 2 * rk_a
    rw_cols = o_gd + rk_g
    wdp, adp = _roundup(2 * rk_w, LANES), _roundup(2 * rk_a, LANES)
    unit = 512 if all(n % 512 == 0 for n in (wg, d)) else 2 * LANES
    cpw = _roundup(o_wd + wdp + adp + rk_g, unit)
    gdp = cpw - (o_wd + wdp + adp)

    def rw_layout(a):
        return jnp.concatenate([a[:, :o_wd], _pad_cols(a[:, o_wd:o_ad], wdp), _pad_cols(a[:, o_ad:o_gd], adp),
                                _pad_cols(a[:, o_gd:rw_cols], gdp)], axis=1)

    o_cv = rw_cols
    o_z = o_cv + 3 * wg
    o_ab = o_z + wg
    o_gt = o_ab + 4 * hg
    c_cv = cpw
    c_ab = c_cv + 3 * wg
    c_z = c_ab + unit
    c_gt = c_z + wg
    w_all = jnp.concatenate([rw_layout(w_in0[:, :rw_cols]), w_in0[:, o_cv:o_z], _pad_cols(w_in0[:, o_ab:o_gt], unit),
                             w_in0[:, o_z:o_ab], w_in0[:, o_gt:]], axis=1).astype(BF16)
    mu_p = rw_layout(rw_mu)

    w2f = jnp.zeros((wdp, 2 * wr), F32)
    a2f = jnp.zeros((adp, 2 * wr), F32)
    for dd in range(2):
        w2f = w2f.at[dd * rk_w:(dd + 1) * rk_w, dd * wr:(dd + 1) * wr].set(rw_w2[0, dd])
        a2f = a2f.at[dd * rk_a:(dd + 1) * rk_a, dd * wr:(dd + 1) * wr].set(rw_a2[0, dd])
    g2p = jnp.pad(rw_g2[0], ((0, gdp - rk_g), (0, 0)))
    li = jnp.arange(LANES)
    e64 = ((li[:, None] // HALF) == (li[None, :] // HALF)).astype(BF16)
    prm = dict(wr=wr, wdp=wdp, adp=adp, cpw=cpw, mu=mu_p,
               w0=rw_w0[0].reshape(1, 2 * wr), w2=w2f.astype(BF16),
               a0=rw_a0[0].reshape(1, 2 * wr), a2=a2f.astype(BF16), g2=g2p.astype(BF16),
               k_k=rw_k_k, k_a=rw_k_a, r_k=rw_r_k[0].reshape(1, wr), e64=e64)

    nega = _pad_cols((-jnp.exp(gdn_a_log[0])).reshape(1, 2 * hg), LANES)
    dtb = _pad_cols(gdn_dt_bias[0].reshape(1, 2 * hg), LANES)

    rows = _roundup(bsz + 1, 8)
    cc = jnp.concatenate([c, c_ctx[None, :], jnp.zeros((rows - bsz - 1, d), F32)], axis=0)
    mod = _ada(cc, w_ada[0], b_ada)
    mods = [mod[:bsz, i * d:(i + 1) * d].reshape(bsz, 1, d) for i in range(6)]
    sh1, sc1, gt1, sh2, sc2, gt2 = mods
    csh1 = mod[bsz:bsz + 1, 0:d].reshape(1, 1, d)
    csc1 = mod[bsz:bsz + 1, d:2 * d].reshape(1, 1, d)

    x2 = x.reshape(bsz * t_len, d)
    ctx2 = ctx.reshape(bsz * tc_len, d)
    proj = _inproj(x2, norm1_w, sh1, sc1, w_all, w_all.shape[1], unit, t_len)
    proj_c = _inproj(ctx2, norm1_w, csh1, csc1, w_all, c_z, unit, bsz * tc_len)

    def mixer_inputs(p, seq_len, latent):
        feat = _rwfeat(p, prm, bsz, seq_len, latent)
        qkv = _gconv(p, c_cv, 3 * wg, unit, gdn_conv_w[0], bsz, seq_len, hg, dg ** -0.5)
        gb = _gfeat(p, c_ab, nega, dtb, hg)
        nc = seq_len // CHUNK
        grows = gb[:, :4 * hg].reshape(bsz, nc, CHUNK, 4, hg).transpose(0, 4, 1, 3, 2)
        return feat, qkv, grows

    feat_c, qkv_c, grows_c = mixer_inputs(proj_c, tc_len, False)
    feat_l, qkv_l, grows_l = mixer_inputs(proj, t_len, True)

    def run_mixer(prep_c, prep_l, nseq):
        s0 = jnp.zeros((2, nseq, LANES, LANES), F32)
        (s_ctx,) = _scan(prep_c[:3], prep_c[3:], s0, False)
        o_f, o_r, _ = _scan(prep_l[:5], prep_l[5:], s_ctx, True)
        return o_f, o_r

    o_f, o_r = run_mixer(_rwprep(feat_c, bsz, tc_len, False), _rwprep(feat_l, bsz, t_len, True),
                         bsz * (wr // LANES))
    o_a = _rw_readout(o_f, o_r, feat_l[7], feat_l[6], rw_ln_w, rw_ln_b, e64, bsz)

    o_f, o_r = run_mixer(_gprep(qkv_c, grows_c, bsz, tc_len, hg, False),
                         _gprep(qkv_l, grows_l, bsz, t_len, hg, True), bsz * hg)
    o_b = _g_readout(o_f, o_r, proj, c_z, gdn_norm_w, bsz)

    mm = _merge(o_a, o_b, merge_p_a[0].astype(BF16), merge_p_b[0].astype(BF16), proj, c_gt, unit)
    x1 = _outproj(mm, w_out[0].astype(BF16), x2, gt1, t_len)
    act = _ffn_up(x1, norm2_w, sh2, sc2, ffn_w_gate_up[0].astype(BF16), t_len)
    out = _ffn_down(act, ffn_w_down[0].astype(BF16), x1, gt2, final_norm_w.reshape(1, d), t_len)
    return out.reshape(bsz, t_len, d)
```

```python
import functools
import math

import jax
import jax.numpy as jnp
from jax import lax
from jax.experimental import pallas as pl
from jax.experimental.pallas import tpu as pltpu

F32 = jnp.float32
BF16 = jnp.bfloat16

GRID_W = 64
NORM_EPS = 1e-6
RW_GN_EPS = 64e-5
CHUNK = 64
LANES = 128
HALF = 64
VMEM_LIMIT = 48 * 1024 * 1024
VMEM_LIMIT_BIG = 58 * 1024 * 1024


def _pick(n, pref, mult):
    t = min(pref, n)
    t -= t % mult
    while t > mult and n % t:
        t -= mult
    assert t >= mult and n % t == 0, (n, pref, mult)
    return t


def _cp(*sem, vmem=VMEM_LIMIT):
    return pltpu.CompilerParams(dimension_semantics=sem, vmem_limit_bytes=vmem)


def _sigmoid(x):
    return 1.0 / (1.0 + jnp.exp(-x))


def _silu(x):
    return x * _sigmoid(x)


def _softplus(x):
    return jnp.maximum(x, 0.0) + jnp.log(1.0 + jnp.exp(-jnp.abs(x)))


def _split(x):
    hi = x.astype(BF16)
    lo = (x - hi.astype(F32)).astype(BF16)
    return hi, lo


_NN = (((1,), (0,)), ((), ()))
_NT = (((1,), (1,)), ((), ()))
_TN = (((0,), (0,)), ((), ()))


def _dg(a, b, dn):
    return lax.dot_general(a, b, dn, preferred_element_type=F32)


def _mm(a, b, dn=_NN, passes=1):
    if passes == 1:
        return _dg(a.astype(BF16), b.astype(BF16), dn)
    ah, al = _split(a)
    bh, bl = _split(b)
    return _dg(ah, bh, dn) + (_dg(ah, bl, dn) + _dg(al, bh, dn))


def _mm_exact_lhs(a_bf, b, dn=_NN):
    bh, bl = _split(b)
    return _dg(a_bf, bh, dn) + _dg(a_bf, bl, dn)


def _iota(shape, dim):
    return lax.broadcasted_iota(jnp.int32, shape, dim)


def _ada_kernel(c_ref, w_ref, b_ref, o_ref):
    s = _silu(c_ref[...])
    o_ref[...] = _dg(s.astype(BF16), w_ref[...].astype(BF16), _NN) + b_ref[...]


def _ada(cc, w, b):
    m, d = cc.shape
    n = w.shape[1]
    tn = _pick(n, 1024, LANES)
    return pl.pallas_call(
        _ada_kernel, grid=(n // tn,),
        in_specs=[pl.BlockSpec((m, d), lambda j: (0, 0)),
                  pl.BlockSpec((d, tn), lambda j: (0, j)),
                  pl.BlockSpec((1, tn), lambda j: (0, j))],
        out_specs=pl.BlockSpec((m, tn), lambda j: (0, j)),
        out_shape=jax.ShapeDtypeStruct((m, n), F32),
        compiler_params=_cp("parallel"), name="ada")(cc, w, b)


def _normmod_rows(x, nw, sh, sc):
    y = x * lax.rsqrt(jnp.mean(x * x, axis=-1, keepdims=True) + NORM_EPS) * nw
    return (y * (1.0 + sc) + sh).astype(BF16)


INPROJ_TN_MAX = 1280


def _inproj_kernel(x_ref, nw_ref, sh_ref, sc_ref, w_ref, o_ref, h_scr):
    @pl.when(pl.program_id(1) == 0)
    def _():
        h_scr[...] = _normmod_rows(x_ref[...], nw_ref[...], sh_ref[...], sc_ref[...])

    o_ref[...] = _dg(h_scr[...], w_ref[...], _NN).astype(o_ref.dtype)


def _inproj(x2, nw, sh, sc, w, ncols, tn, rows_per_mod):
    m, d = x2.shape
    tm = _pick(rows_per_mod, 1024, 8)
    per = rows_per_mod // tm
    return pl.pallas_call(
        _inproj_kernel, grid=(m // tm, ncols // tn),
        in_specs=[pl.BlockSpec((tm, d), lambda i, j: (i, 0)),
                  pl.BlockSpec((1, d), lambda i, j: (0, 0)),
                  pl.BlockSpec((None, 1, d), lambda i, j: (i // per, 0, 0)),
                  pl.BlockSpec((None, 1, d), lambda i, j: (i // per, 0, 0)),
                  pl.BlockSpec((d, tn), lambda i, j: (0, j))],
        out_specs=pl.BlockSpec((tm, tn), lambda i, j: (i, j)),
        out_shape=jax.ShapeDtypeStruct((m, ncols), BF16),
        scratch_shapes=[pltpu.VMEM((tm, d), BF16)],
        compiler_params=_cp("parallel", "arbitrary"), name="inproj")(x2, nw, sh, sc, w)


def _shiftmix(mode, t_len, i, p, prev, nxt_rows, mu):
    shp = p.shape
    tt = shp[0]
    t = i * tt + _iota(shp, 0)
    lane = _iota(shp, 1)
    if mode == "grid":
        cl = lane & 3
        w = t & (GRID_W - 1)
        left = jnp.where(w != 0, pltpu.roll(p, 1, 0), 0.0)
        right = jnp.where(w != GRID_W - 1, pltpu.roll(p, tt - 1, 0), 0.0)
        if tt > GRID_W:
            up = jnp.concatenate([prev, p[:tt - GRID_W]], axis=0)
            down = jnp.concatenate([p[GRID_W:], nxt_rows], axis=0)
        else:
            up, down = prev, nxt_rows
        up = jnp.where(t >= GRID_W, up, 0.0)
        down = jnp.where(t < t_len - GRID_W, down, 0.0)
        sh = jnp.where(cl == 0, left, jnp.where(cl == 1, right, jnp.where(cl == 2, up, down)))
    else:
        prv = jnp.where(t != 0, pltpu.roll(p, 1, 0), 0.0)
        nxt = jnp.where(t != t_len - 1, pltpu.roll(p, tt - 1, 0), 0.0)
        sh = jnp.where((lane & 1) == 0, prv, nxt)
    return p + (sh - p) * mu


def _segsum(x, e):
    outs = []
    for g in range(x.shape[1] // LANES):
        outs.append(_mm_exact_lhs_rhs(x[:, g * LANES:(g + 1) * LANES], e))
    return outs[0] if len(outs) == 1 else jnp.concatenate(outs, axis=1)


def _mm_exact_lhs_rhs(x, e_bf):
    hi, lo = _split(x)
    return _dg(hi, e_bf, _NN) + _dg(lo, e_bf, _NN)


def _rwfeat_kernel(wr, wdp, adp, latent, t_len, p_ref, prev_ref, next_ref, mu_ref, w0_ref, w2_ref, a0_ref, a2_ref,
                   g2_ref, kk_ref_, ka_ref, rk_ref, e_ref, *outs):
    if latent:
        r_o, v_o, kk_o, lw_o, kd_o, bd_o, gate_o, bonus_o = outs
    else:
        r_o, v_o, kk_o, lw_o, kd_o, bd_o = outs
    e = e_ref[...]
    xs = _shiftmix("grid" if latent else "seq", t_len, pl.program_id(1), p_ref[...].astype(F32),
                   prev_ref[...].astype(F32), next_ref[...].astype(F32), mu_ref[...])
    r = xs[:, 0:wr]
    k = xs[:, wr:2 * wr]
    v = xs[:, 2 * wr:3 * wr]
    wd = xs[:, 3 * wr:3 * wr + wdp]
    ad = xs[:, 3 * wr + wdp:3 * wr + wdp + adp]
    lw = _dg(jnp.tanh(wd).astype(BF16), w2_ref[...], _NN) + w0_ref[...]
    w_log = -_softplus(-lw) - 0.5
    logw = -jnp.exp(w_log)
    iclr = _sigmoid(_dg(ad.astype(BF16), a2_ref[...], _NN) + a0_ref[...])
    kk0 = k * kk_ref_[...]
    kk = kk0 * lax.rsqrt(_segsum(kk0 * kk0, e) + NORM_EPS)
    r_o[...] = r.astype(BF16)
    v_o[...] = v.astype(BF16)
    kk_o[...] = kk.astype(BF16)
    ka = ka_ref[...]
    kdirs = []
    for d in range(2):
        ic = iclr[:, d * wr:(d + 1) * wr]
        kd = k * (1.0 + (ic - 1.0) * ka)
        kdirs.append(kd)
        lw_o[d] = logw[:, d * wr:(d + 1) * wr]
        kd_o[d] = kd.astype(BF16)
        bd_o[d] = (kk * ic).astype(BF16)
    if latent:
        gd = xs[:, 3 * wr + wdp + adp:]
        gate_o[...] = _dg(_sigmoid(gd).astype(BF16), g2_ref[...], _NN).astype(BF16)
        kb = 0.5 * (kdirs[0] + kdirs[1])
        bonus_o[...] = (_segsum(r * kb * rk_ref[...], e) * v).astype(BF16)


def _rwfeat(p, prm, bsz, t_len, latent):
    m = bsz * t_len
    wr, wdp, adp, cpw = prm["wr"], prm["wdp"], prm["adp"], prm["cpw"]
    tm = _pick(t_len, 256, GRID_W) if latent else t_len
    nt = t_len // tm
    hb = tm // GRID_W
    nh = m // GRID_W
    full = lambda b, i: (0, 0)
    names = ("mu", "w0", "w2", "a0", "a2", "g2", "k_k", "k_a", "r_k", "e64")
    w_specs = [pl.BlockSpec(prm[n].shape, full) for n in names]
    tok = pl.BlockSpec((tm, wr), lambda b, i: (b * nt + i, 0))
    tok2 = pl.BlockSpec((2, tm, wr), lambda b, i: (0, b * nt + i, 0))
    s1 = jax.ShapeDtypeStruct((m, wr), BF16)
    s2 = jax.ShapeDtypeStruct((2, m, wr), BF16)
    out_specs = [tok, tok, tok, tok2, tok2, tok2]
    out_shape = [s1, s1, s1, jax.ShapeDtypeStruct((2, m, wr), F32), s2, s2]
    if latent:
        out_specs += [tok, tok]
        out_shape += [s1, s1]
    kern = functools.partial(_rwfeat_kernel, wr, wdp, adp, latent, t_len)
    prev = pl.BlockSpec((GRID_W, cpw), lambda b, i: (jnp.maximum((b * nt + i) * hb - 1, 0), 0))
    nxt = pl.BlockSpec((GRID_W, cpw), lambda b, i: (jnp.minimum((b * nt + i + 1) * hb, nh - 1), 0))
    return pl.pallas_call(
        kern, grid=(bsz, nt),
        in_specs=[pl.BlockSpec((tm, cpw), lambda b, i: (b * nt + i, 0)), prev, nxt] + w_specs,
        out_specs=out_specs, out_shape=out_shape,
        compiler_params=_cp("parallel", "parallel"), name="rwfeat")(p, p, p, *[prm[n] for n in names])


INV_PASSES = 1
PREP_PASSES = 1
PREP_CHAINS = 16


def _prep_unroll(tc, chains_per_chunk):
    u = PREP_CHAINS // chains_per_chunk
    while tc % u:
        u //= 2
    return u


def _stack(x, h0):
    return jnp.concatenate([jnp.where(h0, x, 0.0), jnp.where(h0, 0.0, x)], axis=0)


def _tri_inverse_many(mats, eye2, h0):
    c = CHUNK
    steps = int(math.log2(c)) - 1
    ps = [eye2 - a for a in mats]
    xs = [_mm(a, _stack(a, h0), passes=INV_PASSES) for a in mats]
    for i in range(steps):
        last = i == steps - 1
        for n in range(len(mats)):
            rhs = _stack(xs[n], h0)
            if last:
                ps[n] = ps[n] + _mm(ps[n], rhs, passes=INV_PASSES)
            else:
                both = _mm(jnp.concatenate([ps[n], xs[n]], axis=0), rhs, passes=INV_PASSES)
                ps[n] = ps[n] + both[:c]
                xs[n] = both[c:]
    return ps


def _rwprep_chunks(need_out, insts):
    c = CHUNK
    shp = (c, LANES)
    row = _iota(shp, 0)
    lane = _iota(shp, 1)
    col = lane & (HALF - 1)
    h0 = lane < HALF
    eye2 = jnp.where(col == row, 1.0, 0.0)
    bdm = (_iota((LANES, LANES), 0) < HALF) == (_iota((LANES, LANES), 1) < HALF)
    cc_r, cc_c = _iota((c, c), 0), _iota((c, c), 1)
    tri_bf = {False: jnp.where(cc_c <= cc_r, 1.0, 0.0).astype(BF16),
              True: jnp.where(cc_c >= cc_r, 1.0, 0.0).astype(BF16)}
    strict = {False: col < row, True: col > row}
    incl = {False: col <= row, True: col >= row}
    n = len(insts)
    gs = [_mm_exact_lhs(tri_bf[rev], lw) for (rev, r, v, kk, lw, k, bb) in insts]
    st = []
    for (rev, r, v, kk, lw, k, bb), g in zip(insts, gs):
        last, mid = (0, c // 2) if rev else (c - 1, c // 2 - 1)
        gm = g - lw
        gtot = g[last:last + 1, :]
        rho = g[mid:mid + 1, :]
        e_inv = jnp.exp(rho - g)
        kkd = kk * jnp.exp(gm - rho)
        et = jnp.exp(gtot - g)
        d = dict(rev=rev, r=r, v=v, g=g, kt=k * et, bbt=bb * et, pc=jnp.exp(gtot), kkd0=kk * jnp.exp(gm))
        lhs = jnp.concatenate([kkd, r * jnp.exp(g - rho)], axis=0) if need_out else kkd
        rhs = jnp.concatenate([_stack(bb * e_inv, h0), _stack(k * e_inv, h0)], axis=0)
        d["ab"] = _mm(lhs, rhs, _NT, PREP_PASSES)
        st.append(d)
    for d in st:
        ab = d["ab"]
        d["a"] = jnp.where(strict[d["rev"]], ab[:c, :LANES], 0.0)
        bm = jnp.where(strict[d["rev"]], ab[:c, LANES:], 0.0)
        if need_out:
            bbr = _mm(jnp.concatenate([bm, jnp.where(incl[d["rev"]], ab[c:, LANES:], 0.0)], axis=0),
                      _stack(d["v"], h0), passes=PREP_PASSES)
            d["bv"], d["brv"] = bbr[:c], bbr[c:]
        else:
            d["bv"] = _mm(bm, _stack(d["v"], h0), passes=PREP_PASSES)
    ts = _tri_inverse_many([d["a"] for d in st], eye2, h0)
    for d, t in zip(st, ts):
        wu = _mm(t, jnp.concatenate([_stack(d["kkd0"], h0), _stack(d["bv"], h0)], axis=1), passes=PREP_PASSES)
        d["w"], d["u0"] = wu[:, :LANES], wu[:, LANES:]
    out = []
    for d in st:
        w, u0, bbt = d["w"], d["u0"], d["bbt"]
        mn = jnp.where(bdm, _mm(w, bbt, _TN, PREP_PASSES), 0.0)
        nnt = jnp.where(bdm, _mm(jnp.concatenate([d["v"], u0], axis=0), jnp.concatenate([d["kt"], -bbt], axis=0),
                                 _TN, PREP_PASSES), 0.0)
        if not need_out:
            out.append((None, None, mn, nnt, d["pc"]))
            continue
        ar = jnp.where(incl[d["rev"]], d["ab"][c:, :LANES], 0.0)
        arwu = _mm(ar, jnp.concatenate([_stack(w, h0), _stack(u0, h0)], axis=1), passes=PREP_PASSES)
        qp = d["r"] * jnp.exp(d["g"]) - arwu[:, :LANES]
        o0 = d["brv"] - arwu[:, LANES:]
        out.append((qp, o0, mn, nnt, d["pc"]))
    return out


def _rwprep_kernel(tc, need_out, r_ref, v_ref, kk_ref, lwf_ref, lwr_ref, kf_ref, kr_ref, bf_ref, br_ref, *outs):
    per = 5 if need_out else 3
    outs_d = (outs[:per], outs[per:])
    ins_d = ((lwf_ref, kf_ref, bf_ref), (lwr_ref, kr_ref, br_ref))

    unroll = _prep_unroll(tc, 2)

    def body(it, carry):
        insts, where = [], []
        for u in range(unroll):
            ci = it * unroll + u
            sl = pl.ds(pl.multiple_of(ci * CHUNK, CHUNK), CHUNK)
            r, v, kk = (ref[sl, :].astype(F32) for ref in (r_ref, v_ref, kk_ref))
            for d in range(2):
                lw_ref, k_ref, b_ref = ins_d[d]
                insts.append((d == 1, r, v, kk, lw_ref[sl, :], k_ref[sl, :].astype(F32),
                              b_ref[sl, :].astype(F32)))
                where.append((d, ci, sl))
        for (d, ci, sl), (qp, o0, mn, nnt, pc) in zip(where, _rwprep_chunks(need_out, insts)):
            o = outs_d[d]
            if need_out:
                o[0][sl, :] = qp.astype(BF16)
                o[1][sl, :] = o0.astype(BF16)
                o = o[2:]
            o[0][ci] = mn.astype(BF16)
            o[1][ci] = nnt.astype(BF16)
            o[2][pl.ds(ci, 1), :] = pc
        return carry

    lax.fori_loop(0, tc // unroll, body, 0)


def _prep_out(nseq, t_len, need_out):
    nc = t_len // CHUNK
    shapes = []
    if need_out:
        shapes += [jax.ShapeDtypeStruct((nseq, t_len, LANES), BF16)] * 2
    shapes += [jax.ShapeDtypeStruct((nseq, nc, LANES, LANES), BF16)] * 2
    shapes += [jax.ShapeDtypeStruct((nseq, nc, LANES), F32)]
    return shapes


def _prep_out_specs(tc, need_out, seq_of):
    specs = []
    if need_out:
        specs += [pl.BlockSpec((None, tc * CHUNK, LANES), lambda b, p, j: (seq_of(b, p), j, 0))] * 2
    specs += [pl.BlockSpec((None, tc, LANES, LANES), lambda b, p, j: (seq_of(b, p), j, 0, 0))] * 2
    specs += [pl.BlockSpec((None, tc, LANES), lambda b, p, j: (seq_of(b, p), j, 0))]
    return specs


def _chunks_per_step(nc, pref=8):
    return pref if nc % pref == 0 else (8 if nc % 8 == 0 else nc)


def _rwprep(feat, bsz, t_len, need_out):
    r, v, kk, lw, kd, bd = feat[:6]
    wr = r.shape[1]
    npair = wr // LANES
    nc = t_len // CHUNK
    tc = _chunks_per_step(nc)
    nblk = nc // tc
    tok = pl.BlockSpec((tc * CHUNK, LANES), lambda b, p, j: (b * nblk + j, p))
    tok_d = lambda d: pl.BlockSpec((None, tc * CHUNK, LANES), lambda b, p, j: (d, b * nblk + j, p))
    seq_of = lambda b, p: b * npair + p
    kern = functools.partial(_rwprep_kernel, tc, need_out)
    return pl.pallas_call(
        kern, grid=(bsz, npair, nblk),
        in_specs=[tok, tok, tok, tok_d(0), tok_d(1), tok_d(0), tok_d(1), tok_d(0), tok_d(1)],
        out_specs=_prep_out_specs(tc, need_out, seq_of) * 2,
        out_shape=_prep_out(bsz * npair, t_len, need_out) * 2,
        compiler_params=_cp("parallel", "parallel", "parallel"), name="rwprep")(
            r, v, kk, lw, lw, kd, kd, bd, bd)


SCAN_GROUP = 8
SCAN_CHUNKS = 8
SCAN_UNROLL = 4


def _scan_kernel(tc, ng, need_out, *refs):
    per = 5 if need_out else 3
    ins = (refs[:per], refs[per:2 * per])
    s0_ref = refs[2 * per]
    rest = refs[2 * per + 1:]
    if need_out:
        o_refs = rest[:2]
        rest = rest[2:]
    sfin_ref, s_scr = rest
    j = pl.program_id(1)

    @pl.when(j == 0)
    def _():
        s_scr[...] = s0_ref[...]

    unroll = _pick(tc, SCAN_UNROLL, 1)

    def body(it, carry):
        states = [[s_scr[d, g] for d in range(2)] for g in range(ng)]
        chains = [(g, d) for g in range(ng) for d in range(2)]
        st_refs = tuple(r[2:] if need_out else r for r in ins)
        for u in range(unroll):
            i = it * unroll + u
            cis = (i, tc - 1 - i)
            s_bf = {c: states[c[0]][c[1]].astype(BF16) for c in chains}
            upd = {(g, d): _dg(s_bf[(g, d)], st_refs[d][0][g, cis[d]], _NN) for (g, d) in chains}
            if need_out:
                for (g, d) in chains:
                    sl = pl.ds(pl.multiple_of(cis[d] * CHUNK, CHUNK), CHUNK)
                    o = _dg(ins[d][0][g, sl, :], s_bf[(g, d)], _NT) + ins[d][1][g, sl, :].astype(F32)
                    o_refs[d][g, sl, :] = o.astype(o_refs[d].dtype)
            for (g, d) in chains:
                states[g][d] = (states[g][d] * st_refs[d][2][g, cis[d]] - upd[(g, d)]
                                + st_refs[d][1][g, cis[d]].astype(F32))
        for g in range(ng):
            for d in range(2):
                s_scr[d, g] = states[g][d]
        return carry

    lax.fori_loop(0, tc // unroll, body, 0)

    @pl.when(j == pl.num_programs(1) - 1)
    def _():
        sfin_ref[...] = s_scr[...]


def _scan(prep_f, prep_r, s0, need_out):
    mn = prep_f[-3]
    nseq, nc = mn.shape[0], mn.shape[1]
    tc = _pick(nc, SCAN_CHUNKS, 1)
    nblk = nc // tc
    ng = _pick(nseq, SCAN_GROUP, 1)
    prep_f = list(prep_f[:-1]) + [prep_f[-1].reshape(nseq, nc, 1, LANES)]
    prep_r = list(prep_r[:-1]) + [prep_r[-1].reshape(nseq, nc, 1, LANES)]

    def specs(rev):
        jj = (lambda j: nblk - 1 - j) if rev else (lambda j: j)
        sp = []
        if need_out:
            sp += [pl.BlockSpec((ng, tc * CHUNK, LANES), lambda s, j: (s, jj(j), 0))] * 2
        sp += [pl.BlockSpec((ng, tc, LANES, LANES), lambda s, j: (s, jj(j), 0, 0))] * 2
        sp += [pl.BlockSpec((ng, tc, 1, LANES), lambda s, j: (s, jj(j), 0, 0))]
        return sp

    st_spec = pl.BlockSpec((2, ng, LANES, LANES), lambda s, j: (0, s, 0, 0))
    out_specs, out_shape = [], []
    if need_out:
        out_specs += [specs(False)[0], specs(True)[0]]
        out_shape += [jax.ShapeDtypeStruct((nseq, nc * CHUNK, LANES), BF16)] * 2
    out_specs.append(st_spec)
    out_shape.append(jax.ShapeDtypeStruct((2, nseq, LANES, LANES), F32))
    kern = functools.partial(_scan_kernel, tc, ng, need_out)
    return pl.pallas_call(
        kern, grid=(nseq // ng, nblk),
        in_specs=specs(False) + specs(True) + [st_spec],
        out_specs=out_specs, out_shape=out_shape,
        scratch_shapes=[pltpu.VMEM((2, ng, LANES, LANES), F32)],
        compiler_params=_cp("parallel", "arbitrary"), name="scan")(*prep_f, *prep_r, s0)


READOUT_GROUP = 4


def _rw_readout_kernel(of_ref, or_ref, bonus_ref, gate_ref, lnw_ref, lnb_ref, e_ref, o_ref):
    e = e_ref[...]
    inv = 1.0 / HALF
    for g in range(of_ref.shape[0]):
        cs = slice(g * LANES, (g + 1) * LANES)
        y = of_ref[g].astype(F32) + or_ref[g].astype(F32)
        mu = _segsum(y, e) * inv
        yc = y - mu
        var = _segsum(yc * yc, e) * inv
        yn = yc * lax.rsqrt(var + RW_GN_EPS) * lnw_ref[:, cs] + lnb_ref[:, cs]
        o_ref[:, cs] = ((yn + bonus_ref[:, cs].astype(F32)) * gate_ref[:, cs].astype(F32)).astype(o_ref.dtype)


def _rw_readout(o_f, o_r, bonus, gate, lnw, lnb, e64, bsz):
    nseq, t_len, _ = o_f.shape
    npair = nseq // bsz
    ng = _pick(npair, READOUT_GROUP, 1)
    npg = npair // ng
    tt = _pick(t_len, 512, 8)
    nt = t_len // tt
    seq = pl.BlockSpec((ng, tt, LANES), lambda b, p, i: (b * npg + p, i, 0))
    tok = pl.BlockSpec((tt, ng * LANES), lambda b, p, i: (b * nt + i, p))
    par = pl.BlockSpec((1, ng * LANES), lambda b, p, i: (0, p))
    return pl.pallas_call(
        _rw_readout_kernel, grid=(bsz, npg, nt),
        in_specs=[seq, seq, tok, tok, par, par, pl.BlockSpec((LANES, LANES), lambda b, p, i: (0, 0))],
        out_specs=tok, out_shape=jax.ShapeDtypeStruct(bonus.shape, BF16),
        compiler_params=_cp("parallel", "parallel", "parallel"), name="rw_readout")(
            o_f, o_r, bonus, gate, lnw, lnb, e64)


CONV_HALO = 16


def _gconv_kernel(taps, nq, scale, p_ref, prev_ref, next_ref, w_ref, o_ref):
    i = pl.program_id(1)
    cb = pl.program_id(2)
    main = p_ref[...].astype(F32)
    tt = main.shape[0]
    prev = jnp.where(i > 0, prev_ref[...].astype(F32), 0.0)
    nxt = jnp.where(i < pl.num_programs(1) - 1, next_ref[...].astype(F32), 0.0)
    xcat = jnp.concatenate([prev, main, nxt], axis=0)
    n = tt + 2 * CONV_HALO
    half = taps // 2
    acc = None
    for j in range(taps):
        sh = (half - j) % n
        xs = xcat if sh == 0 else pltpu.roll(xcat, sh, 0)
        term = xs[CONV_HALO:CONV_HALO + tt] * w_ref[j:j + 1, :]
        acc = term if acc is None else acc + term
    u = _silu(acc)
    heads = u.shape[1] // LANES
    for s in range(heads):
        hd = cb * heads + s
        us = u[:, s * LANES:(s + 1) * LANES]
        un = us * lax.rsqrt(jnp.sum(us * us, axis=-1, keepdims=True) + NORM_EPS)
        un = un * jnp.where(hd < nq, scale, 1.0)
        o_ref[:, s * LANES:(s + 1) * LANES] = jnp.where(hd < 2 * nq, un, us).astype(o_ref.dtype)


def _gconv(p, col0, c3, cb, conv_w, bsz, t_len, nq, scale):
    m = bsz * t_len
    taps = conv_w.shape[0]
    assert taps // 2 <= CONV_HALO and col0 % cb == 0 and c3 % cb == 0
    tt = _pick(t_len, 512, CONV_HALO)
    nt = t_len // tt
    hb = tt // CONV_HALO
    nh = m // CONV_HALO
    c0 = col0 // cb
    kern = functools.partial(_gconv_kernel, taps, nq, scale)
    return pl.pallas_call(
        kern, grid=(bsz, nt, c3 // cb),
        in_specs=[pl.BlockSpec((tt, cb), lambda b, i, c: (b * nt + i, c0 + c)),
                  pl.BlockSpec((CONV_HALO, cb), lambda b, i, c: (jnp.maximum((b * nt + i) * hb - 1, 0), c0 + c)),
                  pl.BlockSpec((CONV_HALO, cb),
                               lambda b, i, c: (jnp.minimum((b * nt + i + 1) * hb, nh - 1), c0 + c)),
                  pl.BlockSpec((taps, cb), lambda b, i, c: (0, c))],
        out_specs=pl.BlockSpec((tt, cb), lambda b, i, c: (b * nt + i, c)),
        out_shape=jax.ShapeDtypeStruct((m, c3), BF16),
        compiler_params=_cp("parallel", "parallel", "parallel"), name="gconv")(p, p, p, conv_w)


def _gfeat_kernel(hg, ab_ref, nega_ref, dtb_ref, o_ref):
    ab = ab_ref[...].astype(F32)
    tm = ab.shape[0]
    lane = _iota(ab.shape, 1)
    g = jnp.where(lane < 2 * hg, nega_ref[...] * _softplus(ab + dtb_ref[...]), 0.0)
    beta = _sigmoid(ab)
    rr = _iota((tm, tm), 0)
    cc = _iota((tm, tm), 1)
    same = (rr // CHUNK) == (cc // CHUNK)
    lf = jnp.where(same & (cc <= rr), 1.0, 0.0).astype(BF16)
    lr = jnp.where(same & (cc >= rr), 1.0, 0.0).astype(BF16)
    gcf = _mm_exact_lhs(lf, g)
    gcr = _mm_exact_lhs(lr, g)
    gc = jnp.where(lane < hg, gcf, gcr)
    o_ref[...] = jnp.where(lane < 2 * hg, gc, beta)


def _gfeat(p, col0, nega, dtb, hg):
    m = p.shape[0]
    tm = _pick(m, 256, CHUNK)
    c0 = col0 // LANES
    return pl.pallas_call(
        functools.partial(_gfeat_kernel, hg), grid=(m // tm,),
        in_specs=[pl.BlockSpec((tm, LANES), lambda i: (i, c0)),
                  pl.BlockSpec((1, LANES), lambda i: (0, 0)),
                  pl.BlockSpec((1, LANES), lambda i: (0, 0))],
        out_specs=pl.BlockSpec((tm, LANES), lambda i: (i, 0)),
        out_shape=jax.ShapeDtypeStruct((m, LANES), F32),
        compiler_params=_cp("parallel"), name="gfeat")(p, nega, dtb)


def _col_from_row(rowv, eye):
    c = rowv.shape[1]
    return jnp.sum(jnp.where(eye, jnp.broadcast_to(rowv, (c, c)), 0.0), axis=1, keepdims=True)


def _gprep_chunks(need_out, insts):
    c = CHUNK
    shp = (c, LANES)
    row = _iota(shp, 0)
    lane = _iota(shp, 1)
    col = lane & (HALF - 1)
    f0 = lane < HALF
    eye = _iota((c, c), 0) == _iota((c, c), 1)
    ahead = jnp.where(f0, row - col, col - row)
    strict = ahead > 0
    incl = ahead >= 0
    eye2 = jnp.where(col == row, 1.0, 0.0)
    st = []
    for (q, k, v, rows) in insts:
        kk2 = jnp.concatenate([k, k], axis=0)
        lhs = jnp.concatenate([k, q], axis=0) if need_out else k
        st.append(dict(q=q, k=k, v=v, rows=rows, kq=_mm(lhs, kk2, _NT, PREP_PASSES)))
    for d in st:
        rows = d["rows"]
        gr = (rows[0:1, :], rows[1:2, :])
        gcol = [_col_from_row(x, eye) for x in gr]
        bcol = [_col_from_row(rows[2 + dd:3 + dd, :], eye) for dd in range(2)]
        grow2 = jnp.concatenate(gr, axis=1)
        gcol2 = jnp.where(f0, gcol[0], gcol[1])
        bcol2 = jnp.where(f0, bcol[0], bcol[1])
        decay = jnp.where(incl, jnp.exp(jnp.where(incl, gcol2 - grow2, 0.0)), 0.0)
        d.update(gcol=gcol, bcol=bcol, decay=decay, glast=(gr[0][:, c - 1:c], gr[1][:, 0:1]),
                 lmat=jnp.where(strict, d["kq"][:c] * bcol2 * decay, 0.0))
    ts = _tri_inverse_many([d["lmat"] for d in st], eye2, f0)
    for d, t in zip(st, ts):
        k, v, gcol, bcol = d["k"], d["v"], d["gcol"], d["bcol"]
        egc = [jnp.exp(gcol[dd]) for dd in range(2)]
        rhs2 = jnp.concatenate([jnp.concatenate([v * bcol[dd], k * (bcol[dd] * egc[dd])], axis=1)
                                for dd in range(2)], axis=0)
        d["egc"] = egc
        both = _mm(_stack(t, f0), rhs2, passes=PREP_PASSES)
        d["sol"] = [both[:c], both[c:]]
    res = []
    for d in st:
        sol, k, q = d["sol"], d["k"], d["q"]
        if need_out:
            qk = jnp.where(incl, d["kq"][c:] * d["decay"], 0.0)
            both = _mm(_stack(qk, f0), jnp.concatenate(sol, axis=0), passes=PREP_PASSES)
            xx = [both[:c], both[c:]]
        out = []
        for dd in range(2):
            u = sol[dd][:, :LANES]
            wk = sol[dd][:, LANES:]
            ktail = k * jnp.exp(d["glast"][dd] - d["gcol"][dd])
            mn = _mm(wk, ktail, _TN, PREP_PASSES)
            nnt = _mm(u, ktail, _TN, PREP_PASSES)
            pc = jnp.broadcast_to(jnp.exp(d["glast"][dd]), (1, LANES))
            if need_out:
                out.append((q * d["egc"][dd] - xx[dd][:, LANES:], xx[dd][:, :LANES], mn, nnt, pc))
            else:
                out.append((None, None, mn, nnt, pc))
        res.append(out)
    return res


def _gprep_kernel(tc, need_out, q_ref, k_ref, v_ref, rows_ref, *outs):
    per = 5 if need_out else 3
    outs_d = (outs[:per], outs[per:])

    unroll = _prep_unroll(tc, 1)

    def body(it, carry):
        insts, where = [], []
        for u in range(unroll):
            ci = it * unroll + u
            sl = pl.ds(pl.multiple_of(ci * CHUNK, CHUNK), CHUNK)
            q = q_ref[sl, :].astype(F32) if need_out else None
            insts.append((q, k_ref[sl, :].astype(F32), v_ref[sl, :].astype(F32), rows_ref[ci]))
            where.append((ci, sl))
        for (ci, sl), res in zip(where, _gprep_chunks(need_out, insts)):
            for d in range(2):
                qp, o0, mn, nnt, pc = res[d]
                o = outs_d[d]
                if need_out:
                    o[0][sl, :] = qp.astype(BF16)
                    o[1][sl, :] = o0.astype(BF16)
                    o = o[2:]
                o[0][ci] = mn.astype(BF16)
                o[1][ci] = nnt.astype(BF16)
                o[2][pl.ds(ci, 1), :] = pc
        return carry

    lax.fori_loop(0, tc // unroll, body, 0)


def _gprep(qkv, rows, bsz, t_len, hg, need_out):
    nc = t_len // CHUNK
    tc = _chunks_per_step(nc, PREP_CHAINS)
    nblk = nc // tc
    tok = lambda off: pl.BlockSpec((tc * CHUNK, LANES), lambda b, h, j: (b * nblk + j, off + h))
    seq_of = lambda b, h: b * hg + h
    kern = functools.partial(_gprep_kernel, tc, need_out)
    return pl.pallas_call(
        kern, grid=(bsz, hg, nblk),
        in_specs=[tok(0), tok(hg), tok(2 * hg),
                  pl.BlockSpec((None, None, tc, 4, CHUNK), lambda b, h, j: (b, h, j, 0, 0))],
        out_specs=_prep_out_specs(tc, need_out, seq_of) * 2,
        out_shape=_prep_out(bsz * hg, t_len, need_out) * 2,
        compiler_params=_cp("parallel", "parallel", "parallel"), name="gprep")(qkv, qkv, qkv, rows)


def _g_readout_kernel(of_ref, or_ref, z_ref, nw_ref, o_ref):
    for g in range(of_ref.shape[0]):
        cs = slice(g * LANES, (g + 1) * LANES)
        o = of_ref[g].astype(F32) + or_ref[g].astype(F32)
        on = o * lax.rsqrt(jnp.mean(o * o, axis=-1, keepdims=True) + NORM_EPS) * nw_ref[...]
        o_ref[:, cs] = (on * _silu(z_ref[:, cs].astype(F32))).astype(o_ref.dtype)


def _g_readout(o_f, o_r, p, col0, nw, bsz):
    nseq, t_len, _ = o_f.shape
    hg = nseq // bsz
    ng = _pick(hg, READOUT_GROUP, 1)
    npg = hg // ng
    tt = _pick(t_len, 512, 8)
    nt = t_len // tt
    assert col0 % (ng * LANES) == 0
    c0 = col0 // (ng * LANES)
    seq = pl.BlockSpec((ng, tt, LANES), lambda b, h, i: (b * npg + h, i, 0))
    ztok = pl.BlockSpec((tt, ng * LANES), lambda b, h, i: (b * nt + i, c0 + h))
    tok = pl.BlockSpec((tt, ng * LANES), lambda b, h, i: (b * nt + i, h))
    return pl.pallas_call(
        _g_readout_kernel, grid=(bsz, npg, nt),
        in_specs=[seq, seq, ztok, pl.BlockSpec((1, LANES), lambda b, h, i: (0, 0))],
        out_specs=tok, out_shape=jax.ShapeDtypeStruct((bsz * t_len, hg * LANES), BF16),
        compiler_params=_cp("parallel", "parallel", "parallel"), name="g_readout")(o_f, o_r, p, nw)


def _merge_kernel(oa_ref, ob_ref, pa_ref, pb_ref, ga_ref, gb_ref, o_ref):
    a = _dg(oa_ref[...], pa_ref[...], _NN)
    b = _dg(ob_ref[...], pb_ref[...], _NN)
    ga = _sigmoid(ga_ref[...].astype(F32))
    gb = _sigmoid(gb_ref[...].astype(F32))
    o_ref[...] = (ga * a + gb * b).astype(o_ref.dtype)


def _merge(oa, ob, pa, pb, p, col0, tn):
    m, wa = oa.shape
    wb = ob.shape[1]
    d = pa.shape[1]
    tm = _pick(m, 1024, 8)
    assert d % tn == 0 and col0 % tn == 0
    nj = d // tn
    c0 = col0 // tn
    return pl.pallas_call(
        _merge_kernel, grid=(m // tm, nj),
        in_specs=[pl.BlockSpec((tm, wa), lambda i, j: (i, 0)),
                  pl.BlockSpec((tm, wb), lambda i, j: (i, 0)),
                  pl.BlockSpec((wa, tn), lambda i, j: (0, j)),
                  pl.BlockSpec((wb, tn), lambda i, j: (0, j)),
                  pl.BlockSpec((tm, tn), lambda i, j: (i, c0 + j)),
                  pl.BlockSpec((tm, tn), lambda i, j: (i, c0 + nj + j))],
        out_specs=pl.BlockSpec((tm, tn), lambda i, j: (i, j)),
        out_shape=jax.ShapeDtypeStruct((m, d), BF16),
        compiler_params=_cp("parallel", "parallel"), name="merge")(oa, ob, pa, pb, p, p)


def _outproj_kernel(m_ref, w_ref, x_ref, g_ref, o_ref):
    o_ref[...] = x_ref[...] + g_ref[...] * _dg(m_ref[...], w_ref[...], _NN)


def _outproj(mm, w, x2, gate, rows_per_mod):
    m, k = mm.shape
    d = w.shape[1]
    tm = _pick(rows_per_mod, 1024, 8)
    per = rows_per_mod // tm
    tn = _pick(d, 512, LANES)
    return pl.pallas_call(
        _outproj_kernel, grid=(m // tm, d // tn),
        in_specs=[pl.BlockSpec((tm, k), lambda i, j: (i, 0)),
                  pl.BlockSpec((k, tn), lambda i, j: (0, j)),
                  pl.BlockSpec((tm, tn), lambda i, j: (i, j)),
                  pl.BlockSpec((None, 1, tn), lambda i, j: (i // per, 0, j))],
        out_specs=pl.BlockSpec((tm, tn), lambda i, j: (i, j)),
        out_shape=jax.ShapeDtypeStruct((m, d), F32),
        compiler_params=_cp("parallel", "parallel"), name="outproj")(mm, w, x2, gate)


def _ffn_up_kernel(x_ref, nw_ref, sh_ref, sc_ref, wg_ref, wu_ref, o_ref, h_scr):
    @pl.when(pl.program_id(1) == 0)
    def _():
        h_scr[...] = _normmod_rows(x_ref[...], nw_ref[...], sh_ref[...], sc_ref[...])

    h = h_scr[...]
    g = _dg(h, wg_ref[...], _NN)
    u = _dg(h, wu_ref[...], _NN)
    o_ref[...] = (_silu(g) * u).astype(o_ref.dtype)


def _ffn_up(x2, nw, sh, sc, wgu, rows_per_mod):
    m, d = x2.shape
    fh = wgu.shape[1] // 2
    tm = _pick(rows_per_mod, 1024, 8)
    per = rows_per_mod // tm
    tn = _pick(fh, 512, LANES)
    nj = fh // tn
    return pl.pallas_call(
        _ffn_up_kernel, grid=(m // tm, nj),
        in_specs=[pl.BlockSpec((tm, d), lambda i, j: (i, 0)),
                  pl.BlockSpec((1, d), lambda i, j: (0, 0)),
                  pl.BlockSpec((None, 1, d), lambda i, j: (i // per, 0, 0)),
                  pl.BlockSpec((None, 1, d), lambda i, j: (i // per, 0, 0)),
                  pl.BlockSpec((d, tn), lambda i, j: (0, j)),
                  pl.BlockSpec((d, tn), lambda i, j: (0, j + nj))],
        out_specs=pl.BlockSpec((tm, tn), lambda i, j: (i, j)),
        out_shape=jax.ShapeDtypeStruct((m, fh), BF16),
        scratch_shapes=[pltpu.VMEM((tm, d), BF16)],
        compiler_params=_cp("parallel", "arbitrary"), name="ffn_up")(x2, nw, sh, sc, wgu, wgu)


def _ffn_down_kernel(a_ref, w_ref, x_ref, g_ref, fw_ref, o_ref):
    kk = pl.program_id(1)

    @pl.when(kk == 0)
    def _():
        o_ref[...] = jnp.zeros_like(o_ref)

    o_ref[...] += _dg(a_ref[...], w_ref[...], _NN)

    @pl.when(kk == pl.num_programs(1) - 1)
    def _():
        x = x_ref[...] + g_ref[...] * o_ref[...]
        y = x * lax.rsqrt(jnp.mean(x * x, axis=-1, keepdims=True) + NORM_EPS)
        o_ref[...] = y * fw_ref[...]


def _ffn_down(act, w, x2, gate, fw, rows_per_mod):
    m, fh = act.shape
    d = w.shape[1]
    tm = _pick(rows_per_mod, 1024, 8)
    per = rows_per_mod // tm
    tk = _pick(fh, 512, LANES)
    return pl.pallas_call(
        _ffn_down_kernel, grid=(m // tm, fh // tk),
        in_specs=[pl.BlockSpec((tm, tk), lambda i, k: (i, k)),
                  pl.BlockSpec((tk, d), lambda i, k: (k, 0)),
                  pl.BlockSpec((tm, d), lambda i, k: (i, 0)),
                  pl.BlockSpec((None, 1, d), lambda i, k: (i // per, 0, 0)),
                  pl.BlockSpec((1, d), lambda i, k: (0, 0))],
        out_specs=pl.BlockSpec((tm, d), lambda i, k: (i, 0)),
        out_shape=jax.ShapeDtypeStruct((m, d), F32),
        compiler_params=_cp("parallel", "arbitrary", vmem=VMEM_LIMIT_BIG), name="ffn_down")(act, w, x2, gate, fw)


def _pad_cols(a, n):
    return jnp.pad(a, ((0, 0), (0, n - a.shape[1])))


def _roundup(n, m):
    return -(-n // m) * m


def kernel(x, c, ctx, c_ctx, w_ada, b_ada, norm1_w, w_in, rw_mu, rw_w0, rw_w2, rw_a0, rw_a2, rw_g2, rw_k_k, rw_k_a, rw_r_k, rw_ln_w, rw_ln_b, gdn_conv_w, gdn_a_log, gdn_dt_bias, gdn_norm_w, merge_p_a, merge_p_b, w_out, norm2_w, ffn_w_gate_up, ffn_w_down, final_norm_w):
    assert w_ada.shape[0] == 1, "single-layer trunk"
    bsz, t_len, d = x.shape
    tc_len = ctx.shape[1]
    h_rw, n_rw = rw_r_k.shape[1:]
    wr = h_rw * n_rw
    rk_w, rk_a, rk_g = rw_w2.shape[2], rw_a2.shape[2], rw_g2.shape[1]
    hg, dg = gdn_a_log.shape[-1], gdn_norm_w.shape[-1]
    wg = hg * dg
    assert n_rw == HALF and dg == LANES and h_rw % 2 == 0
    assert t_len % GRID_W == 0 and t_len % CHUNK == 0 and tc_len % CHUNK == 0

    w_in0 = w_in[0]
    o_wd = 3 * wr
    o_ad = o_wd + 2 * rk_w
    o_gd = o_ad + 2 * rk_a
    rw_cols = o_gd + rk_g
    wdp, adp = _roundup(2 * rk_w, LANES), _roundup(2 * rk_a, LANES)
    unit = 512 if all(n % 512 == 0 for n in (wg, d)) else 2 * LANES
    cpw = _roundup(o_wd + wdp + adp + rk_g, unit)
    gdp = cpw - (o_wd + wdp + adp)

    def rw_layout(a):
        return jnp.concatenate([a[:, :o_wd], _pad_cols(a[:, o_wd:o_ad], wdp), _pad_cols(a[:, o_ad:o_gd], adp),
                                _pad_cols(a[:, o_gd:rw_cols], gdp)], axis=1)

    o_cv = rw_cols
    o_z = o_cv + 3 * wg
    o_ab = o_z + wg
    o_gt = o_ab + 4 * hg
    c_cv = cpw
    c_ab = c_cv + 3 * wg
    c_z = c_ab + unit
    c_gt = c_z + wg
    w_all = jnp.concatenate([rw_layout(w_in0[:, :rw_cols]), w_in0[:, o_cv:o_z], _pad_cols(w_in0[:, o_ab:o_gt], unit),
                             w_in0[:, o_z:o_ab], w_in0[:, o_gt:]], axis=1).astype(BF16)
    mu_p = rw_layout(rw_mu)

    w2f = jnp.zeros((wdp, 2 * wr), F32)
    a2f = jnp.zeros((adp, 2 * wr), F32)
    for dd in range(2):
        w2f = w2f.at[dd * rk_w:(dd + 1) * rk_w, dd * wr:(dd + 1) * wr].set(rw_w2[0, dd])
        a2f = a2f.at[dd * rk_a:(dd + 1) * rk_a, dd * wr:(dd + 1) * wr].set(rw_a2[0, dd])
    gd_w = _roundup(rk_g, LANES)
    c_need = o_wd + wdp + adp + gd_w
    g2p = jnp.pad(rw_g2[0], ((0, gd_w - rk_g), (0, 0)))
    li = jnp.arange(LANES)
    e64 = ((li[:, None] // HALF) == (li[None, :] // HALF)).astype(BF16)
    prm = dict(wr=wr, wdp=wdp, adp=adp, cpw=c_need, mu=mu_p[:, :c_need],
               w0=rw_w0[0].reshape(1, 2 * wr), w2=w2f.astype(BF16),
               a0=rw_a0[0].reshape(1, 2 * wr), a2=a2f.astype(BF16), g2=g2p.astype(BF16),
               k_k=rw_k_k, k_a=rw_k_a, r_k=rw_r_k[0].reshape(1, wr), e64=e64)

    nega = _pad_cols((-jnp.exp(gdn_a_log[0])).reshape(1, 2 * hg), LANES)
    dtb = _pad_cols(gdn_dt_bias[0].reshape(1, 2 * hg), LANES)

    rows = _roundup(bsz + 1, 8)
    cc = jnp.concatenate([c, c_ctx[None, :], jnp.zeros((rows - bsz - 1, d), F32)], axis=0)
    mod = _ada(cc, w_ada[0], b_ada)
    mods = [mod[:bsz, i * d:(i + 1) * d].reshape(bsz, 1, d) for i in range(6)]
    sh1, sc1, gt1, sh2, sc2, gt2 = mods
    csh1 = mod[bsz:bsz + 1, 0:d].reshape(1, 1, d)
    csc1 = mod[bsz:bsz + 1, d:2 * d].reshape(1, 1, d)

    x2 = x.reshape(bsz * t_len, d)
    ctx2 = ctx.reshape(bsz * tc_len, d)
    n_all = w_all.shape[1]
    tn_in = max(t for t in range(unit, INPROJ_TN_MAX + 1, unit) if n_all % t == 0 and c_z % t == 0)
    proj = _inproj(x2, norm1_w, sh1, sc1, w_all, n_all, tn_in, t_len)
    proj_c = _inproj(ctx2, norm1_w, csh1, csc1, w_all, c_z, tn_in, bsz * tc_len)

    def mixer_inputs(p, seq_len, latent):
        feat = _rwfeat(p, prm, bsz, seq_len, latent)
        qkv = _gconv(p, c_cv, 3 * wg, unit, gdn_conv_w[0], bsz, seq_len, hg, dg ** -0.5)
        gb = _gfeat(p, c_ab, nega, dtb, hg)
        nc = seq_len // CHUNK
        grows = gb[:, :4 * hg].reshape(bsz, nc, CHUNK, 4, hg).transpose(0, 4, 1, 3, 2)
        return feat, qkv, grows

    feat_c, qkv_c, grows_c = mixer_inputs(proj_c, tc_len, False)
    feat_l, qkv_l, grows_l = mixer_inputs(proj, t_len, True)

    def run_mixer(prep_c, prep_l, nseq):
        s0 = jnp.zeros((2, nseq, LANES, LANES), F32)
        (s_ctx,) = _scan(prep_c[:3], prep_c[3:], s0, False)
        o_f, o_r, _ = _scan(prep_l[:5], prep_l[5:], s_ctx, True)
        return o_f, o_r

    o_f, o_r = run_mixer(_rwprep(feat_c, bsz, tc_len, False), _rwprep(feat_l, bsz, t_len, True),
                         bsz * (wr // LANES))
    o_a = _rw_readout(o_f, o_r, feat_l[7], feat_l[6], rw_ln_w, rw_ln_b, e64, bsz)

    o_f, o_r = run_mixer(_gprep(qkv_c, grows_c, bsz, tc_len, hg, False),
                         _gprep(qkv_l, grows_l, bsz, t_len, hg, True), bsz * hg)
    o_b = _g_readout(o_f, o_r, proj, c_z, gdn_norm_w, bsz)

    mm = _merge(o_a, o_b, merge_p_a[0].astype(BF16), merge_p_b[0].astype(BF16), proj, c_gt, unit)
    x1 = _outproj(mm, w_out[0].astype(BF16), x2, gt1, t_len)
    act = _ffn_up(x1, norm2_w, sh2, sc2, ffn_w_gate_up[0].astype(BF16), t_len)
    out = _ffn_down(act, ffn_w_down[0].astype(BF16), x1, gt2, final_norm_w.reshape(1, d), t_len)
    return out.reshape(bsz, t_len, d)
```

```python
import functools
import math

import jax
import jax.numpy as jnp
from jax import lax
from jax.experimental import pallas as pl
from jax.experimental.pallas import tpu as pltpu

F32 = jnp.float32
BF16 = jnp.bfloat16

GRID_W = 64
NORM_EPS = 1e-6
RW_GN_EPS = 64e-5
CHUNK = 64
LANES = 128
HALF = 64
VMEM_LIMIT = 48 * 1024 * 1024
VMEM_LIMIT_BIG = 58 * 1024 * 1024


def _pick(n, pref, mult):
    t = min(pref, n)
    t -= t % mult
    while t > mult and n % t:
        t -= mult
    assert t >= mult and n % t == 0, (n, pref, mult)
    return t


def _cp(*sem, vmem=VMEM_LIMIT):
    return pltpu.CompilerParams(dimension_semantics=sem, vmem_limit_bytes=vmem)


def _sigmoid(x):
    return 1.0 / (1.0 + jnp.exp(-x))


def _silu(x):
    return x * _sigmoid(x)


def _softplus(x):
    return jnp.maximum(x, 0.0) + jnp.log(1.0 + jnp.exp(-jnp.abs(x)))


def _split(x):
    hi = x.astype(BF16)
    lo = (x - hi.astype(F32)).astype(BF16)
    return hi, lo


_NN = (((1,), (0,)), ((), ()))
_NT = (((1,), (1,)), ((), ()))
_TN = (((0,), (0,)), ((), ()))


def _dg(a, b, dn):
    return lax.dot_general(a, b, dn, preferred_element_type=F32)


def _mm(a, b, dn=_NN, passes=1):
    if passes == 1:
        return _dg(a.astype(BF16), b.astype(BF16), dn)
    ah, al = _split(a)
    bh, bl = _split(b)
    return _dg(ah, bh, dn) + (_dg(ah, bl, dn) + _dg(al, bh, dn))


def _mm_exact_lhs(a_bf, b, dn=_NN):
    bh, bl = _split(b)
    return _dg(a_bf, bh, dn) + _dg(a_bf, bl, dn)


def _iota(shape, dim):
    return lax.broadcasted_iota(jnp.int32, shape, dim)


def _ada_kernel(c_ref, w_ref, b_ref, o_ref):
    s = _silu(c_ref[...])
    o_ref[...] = _dg(s.astype(BF16), w_ref[...].astype(BF16), _NN) + b_ref[...]


def _ada(cc, w, b):
    m, d = cc.shape
    n = w.shape[1]
    tn = _pick(n, 1024, LANES)
    return pl.pallas_call(
        _ada_kernel, grid=(n // tn,),
        in_specs=[pl.BlockSpec((m, d), lambda j: (0, 0)),
                  pl.BlockSpec((d, tn), lambda j: (0, j)),
                  pl.BlockSpec((1, tn), lambda j: (0, j))],
        out_specs=pl.BlockSpec((m, tn), lambda j: (0, j)),
        out_shape=jax.ShapeDtypeStruct((m, n), F32),
        compiler_params=_cp("parallel"), name="ada")(cc, w, b)


def _normmod_rows(x, nw, sh, sc):
    y = x * lax.rsqrt(jnp.mean(x * x, axis=-1, keepdims=True) + NORM_EPS) * nw
    return (y * (1.0 + sc) + sh).astype(BF16)


INPROJ_TN_MAX = 1280


def _inproj_kernel(x_ref, nw_ref, sh_ref, sc_ref, w_ref, o_ref, h_scr):
    @pl.when(pl.program_id(1) == 0)
    def _():
        h_scr[...] = _normmod_rows(x_ref[...], nw_ref[...], sh_ref[...], sc_ref[...])

    o_ref[...] = _dg(h_scr[...], w_ref[...], _NN).astype(o_ref.dtype)


def _inproj(x2, nw, sh, sc, w, ncols, tn, rows_per_mod):
    m, d = x2.shape
    tm = _pick(rows_per_mod, 1024, 8)
    per = rows_per_mod // tm
    return pl.pallas_call(
        _inproj_kernel, grid=(m // tm, ncols // tn),
        in_specs=[pl.BlockSpec((tm, d), lambda i, j: (i, 0)),
                  pl.BlockSpec((1, d), lambda i, j: (0, 0)),
                  pl.BlockSpec((None, 1, d), lambda i, j: (i // per, 0, 0)),
                  pl.BlockSpec((None, 1, d), lambda i, j: (i // per, 0, 0)),
                  pl.BlockSpec((d, tn), lambda i, j: (0, j))],
        out_specs=pl.BlockSpec((tm, tn), lambda i, j: (i, j)),
        out_shape=jax.ShapeDtypeStruct((m, ncols), BF16),
        scratch_shapes=[pltpu.VMEM((tm, d), BF16)],
        compiler_params=_cp("parallel", "arbitrary"), name="inproj")(x2, nw, sh, sc, w)


def _shiftmix(mode, t_len, i, p_bf, prev, nxt_rows, mu):
    p = p_bf.astype(F32)
    shp = p.shape
    tt = shp[0]
    t = i * tt + _iota(shp, 0)
    lane = _iota(shp, 1)
    rr = _iota((tt, tt), 0)
    cc = _iota((tt, tt), 1)
    tg = i * tt + rr
    if mode == "grid":
        ok_l, ok_r = (tg & (GRID_W - 1)) != 0, (tg & (GRID_W - 1)) != GRID_W - 1
    else:
        ok_l, ok_r = tg != 0, tg != t_len - 1
    s_l = jnp.where(cc == rr - 1, jnp.where(ok_l, 1.0, 0.0), 0.0).astype(BF16)
    s_r = jnp.where(cc == rr + 1, jnp.where(ok_r, 1.0, 0.0), 0.0).astype(BF16)
    left = _dg(s_l, p_bf, _NN)
    right = _dg(s_r, p_bf, _NN)
    if mode == "grid":
        cl = lane & 3
        if tt > GRID_W:
            up = jnp.concatenate([prev, p[:tt - GRID_W]], axis=0)
            down = jnp.concatenate([p[GRID_W:], nxt_rows], axis=0)
        else:
            up, down = prev, nxt_rows
        up = jnp.where(t >= GRID_W, up, 0.0)
        down = jnp.where(t < t_len - GRID_W, down, 0.0)
        sh = jnp.where(cl == 0, left, jnp.where(cl == 1, right, jnp.where(cl == 2, up, down)))
    else:
        sh = jnp.where((lane & 1) == 0, left, right)
    return p + (sh - p) * mu


def _segsum(x, e):
    outs = []
    for g in range(x.shape[1] // LANES):
        outs.append(_mm_exact_lhs_rhs(x[:, g * LANES:(g + 1) * LANES], e))
    return outs[0] if len(outs) == 1 else jnp.concatenate(outs, axis=1)


def _mm_exact_lhs_rhs(x, e_bf):
    hi, lo = _split(x)
    return _dg(hi, e_bf, _NN) + _dg(lo, e_bf, _NN)


def _rwfeat_kernel(wr, wdp, adp, latent, t_len, p_ref, prev_ref, next_ref, mu_ref, w0_ref, w2_ref, a0_ref, a2_ref,
                   g2_ref, kk_ref_, ka_ref, rk_ref, e_ref, *outs):
    if latent:
        r_o, v_o, kk_o, lw_o, kd_o, bd_o, gate_o, bonus_o = outs
    else:
        r_o, v_o, kk_o, lw_o, kd_o, bd_o = outs
    e = e_ref[...]
    xs = _shiftmix("grid" if latent else "seq", t_len, pl.program_id(1), p_ref[...],
                   prev_ref[...].astype(F32), next_ref[...].astype(F32), mu_ref[...])
    r = xs[:, 0:wr]
    k = xs[:, wr:2 * wr]
    v = xs[:, 2 * wr:3 * wr]
    wd = xs[:, 3 * wr:3 * wr + wdp]
    ad = xs[:, 3 * wr + wdp:3 * wr + wdp + adp]
    lw = _dg(jnp.tanh(wd).astype(BF16), w2_ref[...], _NN) + w0_ref[...]
    w_log = -_softplus(-lw) - 0.5
    logw = -jnp.exp(w_log)
    iclr = _sigmoid(_dg(ad.astype(BF16), a2_ref[...], _NN) + a0_ref[...])
    kk0 = k * kk_ref_[...]
    kk = kk0 * lax.rsqrt(_segsum(kk0 * kk0, e) + NORM_EPS)
    r_o[...] = r.astype(BF16)
    v_o[...] = v.astype(BF16)
    kk_o[...] = kk.astype(BF16)
    ka = ka_ref[...]
    kdirs = []
    for d in range(2):
        ic = iclr[:, d * wr:(d + 1) * wr]
        kd = k * (1.0 + (ic - 1.0) * ka)
        kdirs.append(kd)
        lw_o[d] = logw[:, d * wr:(d + 1) * wr]
        kd_o[d] = kd.astype(BF16)
        bd_o[d] = (kk * ic).astype(BF16)
    if latent:
        gd = xs[:, 3 * wr + wdp + adp:]
        gate_o[...] = _dg(_sigmoid(gd).astype(BF16), g2_ref[...], _NN).astype(BF16)
        kb = 0.5 * (kdirs[0] + kdirs[1])
        bonus_o[...] = (_segsum(r * kb * rk_ref[...], e) * v).astype(BF16)


def _rwfeat(p, prm, bsz, t_len, latent):
    m = bsz * t_len
    wr, wdp, adp, cpw = prm["wr"], prm["wdp"], prm["adp"], prm["cpw"]
    tm = _pick(t_len, 256, GRID_W) if latent else t_len
    nt = t_len // tm
    hb = tm // GRID_W
    nh = m // GRID_W
    full = lambda b, i: (0, 0)
    names = ("mu", "w0", "w2", "a0", "a2", "g2", "k_k", "k_a", "r_k", "e64")
    w_specs = [pl.BlockSpec(prm[n].shape, full) for n in names]
    tok = pl.BlockSpec((tm, wr), lambda b, i: (b * nt + i, 0))
    tok2 = pl.BlockSpec((2, tm, wr), lambda b, i: (0, b * nt + i, 0))
    s1 = jax.ShapeDtypeStruct((m, wr), BF16)
    s2 = jax.ShapeDtypeStruct((2, m, wr), BF16)
    out_specs = [tok, tok, tok, tok2, tok2, tok2]
    out_shape = [s1, s1, s1, jax.ShapeDtypeStruct((2, m, wr), F32), s2, s2]
    if latent:
        out_specs += [tok, tok]
        out_shape += [s1, s1]
    kern = functools.partial(_rwfeat_kernel, wr, wdp, adp, latent, t_len)
    prev = pl.BlockSpec((GRID_W, cpw), lambda b, i: (jnp.maximum((b * nt + i) * hb - 1, 0), 0))
    nxt = pl.BlockSpec((GRID_W, cpw), lambda b, i: (jnp.minimum((b * nt + i + 1) * hb, nh - 1), 0))
    return pl.pallas_call(
        kern, grid=(bsz, nt),
        in_specs=[pl.BlockSpec((tm, cpw), lambda b, i: (b * nt + i, 0)), prev, nxt] + w_specs,
        out_specs=out_specs, out_shape=out_shape,
        compiler_params=_cp("parallel", "parallel"), name="rwfeat")(p, p, p, *[prm[n] for n in names])


INV_PASSES = 1
PREP_PASSES = 1
PREP_CHAINS = 16


def _prep_unroll(tc, chains_per_chunk):
    u = PREP_CHAINS // chains_per_chunk
    while tc % u:
        u //= 2
    return u


def _stack(x, h0):
    return jnp.concatenate([jnp.where(h0, x, 0.0), jnp.where(h0, 0.0, x)], axis=0)


def _tri_inverse_many(mats, eye2, h0):
    c = CHUNK
    steps = int(math.log2(c)) - 1
    ps = [eye2 - a for a in mats]
    xs = [_mm(a, _stack(a, h0), passes=INV_PASSES) for a in mats]
    for i in range(steps):
        last = i == steps - 1
        for n in range(len(mats)):
            rhs = _stack(xs[n], h0)
            if last:
                ps[n] = ps[n] + _mm(ps[n], rhs, passes=INV_PASSES)
            else:
                both = _mm(jnp.concatenate([ps[n], xs[n]], axis=0), rhs, passes=INV_PASSES)
                ps[n] = ps[n] + both[:c]
                xs[n] = both[c:]
    return ps


def _rwprep_chunks(need_out, insts):
    c = CHUNK
    shp = (c, LANES)
    row = _iota(shp, 0)
    lane = _iota(shp, 1)
    col = lane & (HALF - 1)
    h0 = lane < HALF
    eye2 = jnp.where(col == row, 1.0, 0.0)
    bdm = (_iota((LANES, LANES), 0) < HALF) == (_iota((LANES, LANES), 1) < HALF)
    cc_r, cc_c = _iota((c, c), 0), _iota((c, c), 1)
    tri_bf = {False: jnp.where(cc_c <= cc_r, 1.0, 0.0).astype(BF16),
              True: jnp.where(cc_c >= cc_r, 1.0, 0.0).astype(BF16)}
    strict = {False: col < row, True: col > row}
    incl = {False: col <= row, True: col >= row}
    n = len(insts)
    gs = [_mm_exact_lhs(tri_bf[rev], lw) for (rev, r, v, kk, lw, k, bb) in insts]
    st = []
    for (rev, r, v, kk, lw, k, bb), g in zip(insts, gs):
        last, mid = (0, c // 2) if rev else (c - 1, c // 2 - 1)
        gm = g - lw
        gtot = g[last:last + 1, :]
        rho = g[mid:mid + 1, :]
        e_inv = jnp.exp(rho - g)
        kkd = kk * jnp.exp(gm - rho)
        et = jnp.exp(gtot - g)
        d = dict(rev=rev, r=r, v=v, g=g, kt=k * et, bbt=bb * et, pc=jnp.exp(gtot), kkd0=kk * jnp.exp(gm))
        lhs = jnp.concatenate([kkd, r * jnp.exp(g - rho)], axis=0) if need_out else kkd
        rhs = jnp.concatenate([_stack(bb * e_inv, h0), _stack(k * e_inv, h0)], axis=0)
        d["ab"] = _mm(lhs, rhs, _NT, PREP_PASSES)
        st.append(d)
    for d in st:
        ab = d["ab"]
        d["a"] = jnp.where(strict[d["rev"]], ab[:c, :LANES], 0.0)
        bm = jnp.where(strict[d["rev"]], ab[:c, LANES:], 0.0)
        if need_out:
            bbr = _mm(jnp.concatenate([bm, jnp.where(incl[d["rev"]], ab[c:, LANES:], 0.0)], axis=0),
                      _stack(d["v"], h0), passes=PREP_PASSES)
            d["bv"], d["brv"] = bbr[:c], bbr[c:]
        else:
            d["bv"] = _mm(bm, _stack(d["v"], h0), passes=PREP_PASSES)
    ts = _tri_inverse_many([d["a"] for d in st], eye2, h0)
    for d, t in zip(st, ts):
        wu = _mm(t, jnp.concatenate([_stack(d["kkd0"], h0), _stack(d["bv"], h0)], axis=1), passes=PREP_PASSES)
        d["w"], d["u0"] = wu[:, :LANES], wu[:, LANES:]
    out = []
    for d in st:
        w, u0, bbt = d["w"], d["u0"], d["bbt"]
        mn = jnp.where(bdm, _mm(w, bbt, _TN, PREP_PASSES), 0.0)
        nnt = jnp.where(bdm, _mm(jnp.concatenate([d["v"], u0], axis=0), jnp.concatenate([d["kt"], -bbt], axis=0),
                                 _TN, PREP_PASSES), 0.0)
        if not need_out:
            out.append((None, None, mn, nnt, d["pc"]))
            continue
        ar = jnp.where(incl[d["rev"]], d["ab"][c:, :LANES], 0.0)
        arwu = _mm(ar, jnp.concatenate([_stack(w, h0), _stack(u0, h0)], axis=1), passes=PREP_PASSES)
        qp = d["r"] * jnp.exp(d["g"]) - arwu[:, :LANES]
        o0 = d["brv"] - arwu[:, LANES:]
        out.append((qp, o0, mn, nnt, d["pc"]))
    return out


def _store_prep(o, need_out, g, ci, sl, res):
    qp, o0, mn, nnt, pc = res
    if need_out:
        o[0][g, sl, :] = qp.astype(BF16)
        o[1][g, sl, :] = o0.astype(BF16)
        o = o[2:]
    o[0][g, ci] = mn.astype(BF16)
    o[1][g, ci] = nnt.astype(BF16)
    o[2][g, pl.ds(ci, 1), :] = pc


def _rwprep_kernel(tc, ng, need_out, r_ref, v_ref, kk_ref, lwf_ref, lwr_ref, kf_ref, kr_ref, bf_ref, br_ref, *outs):
    per = 5 if need_out else 3
    outs_d = (outs[:per], outs[per:])
    ins_d = ((lwf_ref, kf_ref, bf_ref), (lwr_ref, kr_ref, br_ref))

    unroll = _prep_unroll(tc, 2 * ng)

    def body(it, carry):
        insts, where = [], []
        for g in range(ng):
            cs = slice(g * LANES, (g + 1) * LANES)
            for u in range(unroll):
                ci = it * unroll + u
                sl = pl.ds(pl.multiple_of(ci * CHUNK, CHUNK), CHUNK)
                r, v, kk = (ref[sl, cs].astype(F32) for ref in (r_ref, v_ref, kk_ref))
                for d in range(2):
                    lw_ref, k_ref, b_ref = ins_d[d]
                    insts.append((d == 1, r, v, kk, lw_ref[sl, cs], k_ref[sl, cs].astype(F32),
                                  b_ref[sl, cs].astype(F32)))
                    where.append((d, g, ci, sl))
        for (d, g, ci, sl), res in zip(where, _rwprep_chunks(need_out, insts)):
            _store_prep(outs_d[d], need_out, g, ci, sl, res)
        return carry

    lax.fori_loop(0, tc // unroll, body, 0)


def _prep_out(nseq, t_len, need_out):
    nc = t_len // CHUNK
    shapes = []
    if need_out:
        shapes += [jax.ShapeDtypeStruct((nseq, t_len, LANES), BF16)] * 2
    shapes += [jax.ShapeDtypeStruct((nseq, nc, LANES, LANES), BF16)] * 2
    shapes += [jax.ShapeDtypeStruct((nseq, nc, LANES), F32)]
    return shapes


def _prep_out_specs(tc, ng, need_out, seq_blk):
    specs = []
    if need_out:
        specs += [pl.BlockSpec((ng, tc * CHUNK, LANES), lambda b, p, j: (seq_blk(b, p), j, 0))] * 2
    specs += [pl.BlockSpec((ng, tc, LANES, LANES), lambda b, p, j: (seq_blk(b, p), j, 0, 0))] * 2
    specs += [pl.BlockSpec((ng, tc, LANES), lambda b, p, j: (seq_blk(b, p), j, 0))]
    return specs


def _chunks_per_step(nc, pref=8):
    return pref if nc % pref == 0 else (8 if nc % 8 == 0 else nc)


def _seqs_per_step(nseq_per_batch, tc, chains_per_chunk):
    return _pick(nseq_per_batch, max(PREP_CHAINS // (tc * chains_per_chunk), 1), 1)


def _rwprep(feat, bsz, t_len, need_out):
    r, v, kk, lw, kd, bd = feat[:6]
    wr = r.shape[1]
    npair = wr // LANES
    nc = t_len // CHUNK
    tc = _chunks_per_step(nc)
    nblk = nc // tc
    ng = _seqs_per_step(npair, tc, 2)
    npg = npair // ng
    tok = pl.BlockSpec((tc * CHUNK, ng * LANES), lambda b, p, j: (b * nblk + j, p))
    tok_d = lambda d: pl.BlockSpec((None, tc * CHUNK, ng * LANES), lambda b, p, j: (d, b * nblk + j, p))
    seq_of = lambda b, p: b * npg + p
    kern = functools.partial(_rwprep_kernel, tc, ng, need_out)
    return pl.pallas_call(
        kern, grid=(bsz, npg, nblk),
        in_specs=[tok, tok, tok, tok_d(0), tok_d(1), tok_d(0), tok_d(1), tok_d(0), tok_d(1)],
        out_specs=_prep_out_specs(tc, ng, need_out, seq_of) * 2,
        out_shape=_prep_out(bsz * npair, t_len, need_out) * 2,
        compiler_params=_cp("parallel", "parallel", "parallel"), name="rwprep")(
            r, v, kk, lw, lw, kd, kd, bd, bd)


SCAN_GROUP = 8
SCAN_CHUNKS = 8
SCAN_UNROLL = 4


def _scan_kernel(tc, ng, need_out, *refs):
    per = 5 if need_out else 3
    ins = (refs[:per], refs[per:2 * per])
    s0_ref = refs[2 * per]
    rest = refs[2 * per + 1:]
    if need_out:
        o_refs = rest[:2]
        rest = rest[2:]
    sfin_ref, s_scr = rest
    j = pl.program_id(1)

    @pl.when(j == 0)
    def _():
        s_scr[...] = s0_ref[...]

    unroll = _pick(tc, SCAN_UNROLL, 1)

    def body(it, carry):
        states = [[s_scr[d, g] for d in range(2)] for g in range(ng)]
        chains = [(g, d) for g in range(ng) for d in range(2)]
        st_refs = tuple(r[2:] if need_out else r for r in ins)
        for u in range(unroll):
            i = it * unroll + u
            cis = (i, tc - 1 - i)
            s_bf = {c: states[c[0]][c[1]].astype(BF16) for c in chains}
            upd = {(g, d): _dg(s_bf[(g, d)], st_refs[d][0][g, cis[d]], _NN) for (g, d) in chains}
            if need_out:
                for (g, d) in chains:
                    sl = pl.ds(pl.multiple_of(cis[d] * CHUNK, CHUNK), CHUNK)
                    o = _dg(ins[d][0][g, sl, :], s_bf[(g, d)], _NT) + ins[d][1][g, sl, :].astype(F32)
                    o_refs[d][g, sl, :] = o.astype(o_refs[d].dtype)
            for (g, d) in chains:
                states[g][d] = (states[g][d] * st_refs[d][2][g, cis[d]] - upd[(g, d)]
                                + st_refs[d][1][g, cis[d]].astype(F32))
        for g in range(ng):
            for d in range(2):
                s_scr[d, g] = states[g][d]
        return carry

    lax.fori_loop(0, tc // unroll, body, 0)

    @pl.when(j == pl.num_programs(1) - 1)
    def _():
        sfin_ref[...] = s_scr[...]


def _scan(prep_f, prep_r, s0, need_out):
    mn = prep_f[-3]
    nseq, nc = mn.shape[0], mn.shape[1]
    tc = _pick(nc, SCAN_CHUNKS, 1)
    nblk = nc // tc
    ng = _pick(nseq, SCAN_GROUP, 1)
    prep_f = list(prep_f[:-1]) + [prep_f[-1].reshape(nseq, nc, 1, LANES)]
    prep_r = list(prep_r[:-1]) + [prep_r[-1].reshape(nseq, nc, 1, LANES)]

    def specs(rev):
        jj = (lambda j: nblk - 1 - j) if rev else (lambda j: j)
        sp = []
        if need_out:
            sp += [pl.BlockSpec((ng, tc * CHUNK, LANES), lambda s, j: (s, jj(j), 0))] * 2
        sp += [pl.BlockSpec((ng, tc, LANES, LANES), lambda s, j: (s, jj(j), 0, 0))] * 2
        sp += [pl.BlockSpec((ng, tc, 1, LANES), lambda s, j: (s, jj(j), 0, 0))]
        return sp

    st_spec = pl.BlockSpec((2, ng, LANES, LANES), lambda s, j: (0, s, 0, 0))
    out_specs, out_shape = [], []
    if need_out:
        out_specs += [specs(False)[0], specs(True)[0]]
        out_shape += [jax.ShapeDtypeStruct((nseq, nc * CHUNK, LANES), BF16)] * 2
    out_specs.append(st_spec)
    out_shape.append(jax.ShapeDtypeStruct((2, nseq, LANES, LANES), F32))
    kern = functools.partial(_scan_kernel, tc, ng, need_out)
    return pl.pallas_call(
        kern, grid=(nseq // ng, nblk),
        in_specs=specs(False) + specs(True) + [st_spec],
        out_specs=out_specs, out_shape=out_shape,
        scratch_shapes=[pltpu.VMEM((2, ng, LANES, LANES), F32)],
        compiler_params=_cp("parallel", "arbitrary"), name="scan")(*prep_f, *prep_r, s0)


READOUT_GROUP = 4


def _rw_readout_kernel(of_ref, or_ref, bonus_ref, gate_ref, lnw_ref, lnb_ref, e_ref, o_ref):
    e = e_ref[...]
    inv = 1.0 / HALF
    for g in range(of_ref.shape[0]):
        cs = slice(g * LANES, (g + 1) * LANES)
        y = of_ref[g].astype(F32) + or_ref[g].astype(F32)
        mu = _segsum(y, e) * inv
        yc = y - mu
        var = _segsum(yc * yc, e) * inv
        yn = yc * lax.rsqrt(var + RW_GN_EPS) * lnw_ref[:, cs] + lnb_ref[:, cs]
        o_ref[:, cs] = ((yn + bonus_ref[:, cs].astype(F32)) * gate_ref[:, cs].astype(F32)).astype(o_ref.dtype)


def _rw_readout(o_f, o_r, bonus, gate, lnw, lnb, e64, bsz):
    nseq, t_len, _ = o_f.shape
    npair = nseq // bsz
    ng = _pick(npair, READOUT_GROUP, 1)
    npg = npair // ng
    tt = _pick(t_len, 512, 8)
    nt = t_len // tt
    seq = pl.BlockSpec((ng, tt, LANES), lambda b, p, i: (b * npg + p, i, 0))
    tok = pl.BlockSpec((tt, ng * LANES), lambda b, p, i: (b * nt + i, p))
    par = pl.BlockSpec((1, ng * LANES), lambda b, p, i: (0, p))
    return pl.pallas_call(
        _rw_readout_kernel, grid=(bsz, npg, nt),
        in_specs=[seq, seq, tok, tok, par, par, pl.BlockSpec((LANES, LANES), lambda b, p, i: (0, 0))],
        out_specs=tok, out_shape=jax.ShapeDtypeStruct(bonus.shape, BF16),
        compiler_params=_cp("parallel", "parallel", "parallel"), name="rw_readout")(
            o_f, o_r, bonus, gate, lnw, lnb, e64)


CONV_HALO = 16


def _gconv_kernel(taps, nq, scale, p_ref, prev_ref, next_ref, w_ref, o_ref):
    i = pl.program_id(1)
    cb = pl.program_id(2)
    main = p_ref[...].astype(F32)
    tt = main.shape[0]
    prev = jnp.where(i > 0, prev_ref[...].astype(F32), 0.0)
    nxt = jnp.where(i < pl.num_programs(1) - 1, next_ref[...].astype(F32), 0.0)
    xcat = jnp.concatenate([prev, main, nxt], axis=0)
    n = tt + 2 * CONV_HALO
    half = taps // 2
    acc = None
    for j in range(taps):
        sh = (half - j) % n
        xs = xcat if sh == 0 else pltpu.roll(xcat, sh, 0)
        term = xs[CONV_HALO:CONV_HALO + tt] * w_ref[j:j + 1, :]
        acc = term if acc is None else acc + term
    u = _silu(acc)
    heads = u.shape[1] // LANES
    for s in range(heads):
        hd = cb * heads + s
        us = u[:, s * LANES:(s + 1) * LANES]
        un = us * lax.rsqrt(jnp.sum(us * us, axis=-1, keepdims=True) + NORM_EPS)
        un = un * jnp.where(hd < nq, scale, 1.0)
        o_ref[:, s * LANES:(s + 1) * LANES] = jnp.where(hd < 2 * nq, un, us).astype(o_ref.dtype)


def _gconv(p, col0, c3, cb, conv_w, bsz, t_len, nq, scale):
    m = bsz * t_len
    taps = conv_w.shape[0]
    assert taps // 2 <= CONV_HALO and col0 % cb == 0 and c3 % cb == 0
    tt = _pick(t_len, 512, CONV_HALO)
    nt = t_len // tt
    hb = tt // CONV_HALO
    nh = m // CONV_HALO
    c0 = col0 // cb
    kern = functools.partial(_gconv_kernel, taps, nq, scale)
    return pl.pallas_call(
        kern, grid=(bsz, nt, c3 // cb),
        in_specs=[pl.BlockSpec((tt, cb), lambda b, i, c: (b * nt + i, c0 + c)),
                  pl.BlockSpec((CONV_HALO, cb), lambda b, i, c: (jnp.maximum((b * nt + i) * hb - 1, 0), c0 + c)),
                  pl.BlockSpec((CONV_HALO, cb),
                               lambda b, i, c: (jnp.minimum((b * nt + i + 1) * hb, nh - 1), c0 + c)),
                  pl.BlockSpec((taps, cb), lambda b, i, c: (0, c))],
        out_specs=pl.BlockSpec((tt, cb), lambda b, i, c: (b * nt + i, c)),
        out_shape=jax.ShapeDtypeStruct((m, c3), BF16),
        compiler_params=_cp("parallel", "parallel", "parallel"), name="gconv")(p, p, p, conv_w)


def _gfeat_kernel(hg, ab_ref, nega_ref, dtb_ref, o_ref):
    ab = ab_ref[...].astype(F32)
    tm = ab.shape[0]
    lane = _iota(ab.shape, 1)
    g = jnp.where(lane < 2 * hg, nega_ref[...] * _softplus(ab + dtb_ref[...]), 0.0)
    beta = _sigmoid(ab)
    rr = _iota((tm, tm), 0)
    cc = _iota((tm, tm), 1)
    same = (rr // CHUNK) == (cc // CHUNK)
    lf = jnp.where(same & (cc <= rr), 1.0, 0.0).astype(BF16)
    lr = jnp.where(same & (cc >= rr), 1.0, 0.0).astype(BF16)
    gcf = _mm_exact_lhs(lf, g)
    gcr = _mm_exact_lhs(lr, g)
    gc = jnp.where(lane < hg, gcf, gcr)
    o_ref[...] = jnp.where(lane < 2 * hg, gc, beta)


def _gfeat(p, col0, nega, dtb, hg):
    m = p.shape[0]
    tm = _pick(m, 256, CHUNK)
    c0 = col0 // LANES
    return pl.pallas_call(
        functools.partial(_gfeat_kernel, hg), grid=(m // tm,),
        in_specs=[pl.BlockSpec((tm, LANES), lambda i: (i, c0)),
                  pl.BlockSpec((1, LANES), lambda i: (0, 0)),
                  pl.BlockSpec((1, LANES), lambda i: (0, 0))],
        out_specs=pl.BlockSpec((tm, LANES), lambda i: (i, 0)),
        out_shape=jax.ShapeDtypeStruct((m, LANES), F32),
        compiler_params=_cp("parallel"), name="gfeat")(p, nega, dtb)


def _col_from_row(rowv, eye):
    c = rowv.shape[1]
    return jnp.sum(jnp.where(eye, jnp.broadcast_to(rowv, (c, c)), 0.0), axis=1, keepdims=True)


def _gprep_chunks(need_out, insts):
    c = CHUNK
    shp = (c, LANES)
    row = _iota(shp, 0)
    lane = _iota(shp, 1)
    col = lane & (HALF - 1)
    f0 = lane < HALF
    eye = _iota((c, c), 0) == _iota((c, c), 1)
    ahead = jnp.where(f0, row - col, col - row)
    strict = ahead > 0
    incl = ahead >= 0
    eye2 = jnp.where(col == row, 1.0, 0.0)
    st = []
    for (q, k, v, rows) in insts:
        kk2 = jnp.concatenate([k, k], axis=0)
        lhs = jnp.concatenate([k, q], axis=0) if need_out else k
        st.append(dict(q=q, k=k, v=v, rows=rows, kq=_mm(lhs, kk2, _NT, PREP_PASSES)))
    for d in st:
        rows = d["rows"]
        gr = (rows[0:1, :], rows[1:2, :])
        gcol = [_col_from_row(x, eye) for x in gr]
        bcol = [_col_from_row(rows[2 + dd:3 + dd, :], eye) for dd in range(2)]
        grow2 = jnp.concatenate(gr, axis=1)
        gcol2 = jnp.where(f0, gcol[0], gcol[1])
        bcol2 = jnp.where(f0, bcol[0], bcol[1])
        decay = jnp.where(incl, jnp.exp(jnp.where(incl, gcol2 - grow2, 0.0)), 0.0)
        d.update(gcol=gcol, bcol=bcol, decay=decay, glast=(gr[0][:, c - 1:c], gr[1][:, 0:1]),
                 lmat=jnp.where(strict, d["kq"][:c] * bcol2 * decay, 0.0))
    ts = _tri_inverse_many([d["lmat"] for d in st], eye2, f0)
    for d, t in zip(st, ts):
        k, v, gcol, bcol = d["k"], d["v"], d["gcol"], d["bcol"]
        egc = [jnp.exp(gcol[dd]) for dd in range(2)]
        rhs2 = jnp.concatenate([jnp.concatenate([v * bcol[dd], k * (bcol[dd] * egc[dd])], axis=1)
                                for dd in range(2)], axis=0)
        d["egc"] = egc
        both = _mm(_stack(t, f0), rhs2, passes=PREP_PASSES)
        d["sol"] = [both[:c], both[c:]]
    res = []
    for d in st:
        sol, k, q = d["sol"], d["k"], d["q"]
        if need_out:
            qk = jnp.where(incl, d["kq"][c:] * d["decay"], 0.0)
            both = _mm(_stack(qk, f0), jnp.concatenate(sol, axis=0), passes=PREP_PASSES)
            xx = [both[:c], both[c:]]
        out = []
        for dd in range(2):
            u = sol[dd][:, :LANES]
            wk = sol[dd][:, LANES:]
            ktail = k * jnp.exp(d["glast"][dd] - d["gcol"][dd])
            mn = _mm(wk, ktail, _TN, PREP_PASSES)
            nnt = _mm(u, ktail, _TN, PREP_PASSES)
            pc = jnp.broadcast_to(jnp.exp(d["glast"][dd]), (1, LANES))
            if need_out:
                out.append((q * d["egc"][dd] - xx[dd][:, LANES:], xx[dd][:, :LANES], mn, nnt, pc))
            else:
                out.append((None, None, mn, nnt, pc))
        res.append(out)
    return res


def _gprep_kernel(tc, ng, need_out, q_ref, k_ref, v_ref, rows_ref, *outs):
    per = 5 if need_out else 3
    outs_d = (outs[:per], outs[per:])

    unroll = _prep_unroll(tc, ng)

    def body(it, carry):
        insts, where = [], []
        for g in range(ng):
            cs = slice(g * LANES, (g + 1) * LANES)
            for u in range(unroll):
                ci = it * unroll + u
                sl = pl.ds(pl.multiple_of(ci * CHUNK, CHUNK), CHUNK)
                q = q_ref[sl, cs].astype(F32) if need_out else None
                insts.append((q, k_ref[sl, cs].astype(F32), v_ref[sl, cs].astype(F32), rows_ref[g, ci]))
                where.append((g, ci, sl))
        for (g, ci, sl), res in zip(where, _gprep_chunks(need_out, insts)):
            for d in range(2):
                _store_prep(outs_d[d], need_out, g, ci, sl, res[d])
        return carry

    lax.fori_loop(0, tc // unroll, body, 0)


def _gprep(qkv, rows, bsz, t_len, hg, need_out):
    nc = t_len // CHUNK
    tc = _chunks_per_step(nc, PREP_CHAINS)
    nblk = nc // tc
    ng = _seqs_per_step(hg, tc, 1)
    npg = hg // ng
    tok = lambda off: pl.BlockSpec((tc * CHUNK, ng * LANES), lambda b, h, j: (b * nblk + j, off * npg + h))
    seq_of = lambda b, h: b * npg + h
    kern = functools.partial(_gprep_kernel, tc, ng, need_out)
    return pl.pallas_call(
        kern, grid=(bsz, npg, nblk),
        in_specs=[tok(0), tok(1), tok(2),
                  pl.BlockSpec((None, ng, tc, 4, CHUNK), lambda b, h, j: (b, h, j, 0, 0))],
        out_specs=_prep_out_specs(tc, ng, need_out, seq_of) * 2,
        out_shape=_prep_out(bsz * hg, t_len, need_out) * 2,
        compiler_params=_cp("parallel", "parallel", "parallel"), name="gprep")(qkv, qkv, qkv, rows)


def _g_readout_kernel(of_ref, or_ref, z_ref, nw_ref, o_ref):
    for g in range(of_ref.shape[0]):
        cs = slice(g * LANES, (g + 1) * LANES)
        o = of_ref[g].astype(F32) + or_ref[g].astype(F32)
        on = o * lax.rsqrt(jnp.mean(o * o, axis=-1, keepdims=True) + NORM_EPS) * nw_ref[...]
        o_ref[:, cs] = (on * _silu(z_ref[:, cs].astype(F32))).astype(o_ref.dtype)


def _g_readout(o_f, o_r, p, col0, nw, bsz):
    nseq, t_len, _ = o_f.shape
    hg = nseq // bsz
    ng = _pick(hg, READOUT_GROUP, 1)
    npg = hg // ng
    tt = _pick(t_len, 512, 8)
    nt = t_len // tt
    assert col0 % (ng * LANES) == 0
    c0 = col0 // (ng * LANES)
    seq = pl.BlockSpec((ng, tt, LANES), lambda b, h, i: (b * npg + h, i, 0))
    ztok = pl.BlockSpec((tt, ng * LANES), lambda b, h, i: (b * nt + i, c0 + h))
    tok = pl.BlockSpec((tt, ng * LANES), lambda b, h, i: (b * nt + i, h))
    return pl.pallas_call(
        _g_readout_kernel, grid=(bsz, npg, nt),
        in_specs=[seq, seq, ztok, pl.BlockSpec((1, LANES), lambda b, h, i: (0, 0))],
        out_specs=tok, out_shape=jax.ShapeDtypeStruct((bsz * t_len, hg * LANES), BF16),
        compiler_params=_cp("parallel", "parallel", "parallel"), name="g_readout")(o_f, o_r, p, nw)


def _merge_kernel(oa_ref, ob_ref, pa_ref, pb_ref, ga_ref, gb_ref, o_ref):
    a = _dg(oa_ref[...], pa_ref[...], _NN)
    b = _dg(ob_ref[...], pb_ref[...], _NN)
    ga = _sigmoid(ga_ref[...].astype(F32))
    gb = _sigmoid(gb_ref[...].astype(F32))
    o_ref[...] = (ga * a + gb * b).astype(o_ref.dtype)


def _merge(oa, ob, pa, pb, p, col0, tn):
    m, wa = oa.shape
    wb = ob.shape[1]
    d = pa.shape[1]
    tm = _pick(m, 1024, 8)
    assert d % tn == 0 and col0 % tn == 0
    nj = d // tn
    c0 = col0 // tn
    return pl.pallas_call(
        _merge_kernel, grid=(m // tm, nj),
        in_specs=[pl.BlockSpec((tm, wa), lambda i, j: (i, 0)),
                  pl.BlockSpec((tm, wb), lambda i, j: (i, 0)),
                  pl.BlockSpec((wa, tn), lambda i, j: (0, j)),
                  pl.BlockSpec((wb, tn), lambda i, j: (0, j)),
                  pl.BlockSpec((tm, tn), lambda i, j: (i, c0 + j)),
                  pl.BlockSpec((tm, tn), lambda i, j: (i, c0 + nj + j))],
        out_specs=pl.BlockSpec((tm, tn), lambda i, j: (i, j)),
        out_shape=jax.ShapeDtypeStruct((m, d), BF16),
        compiler_params=_cp("parallel", "parallel"), name="merge")(oa, ob, pa, pb, p, p)


def _outproj_kernel(m_ref, w_ref, x_ref, g_ref, o_ref):
    o_ref[...] = x_ref[...] + g_ref[...] * _dg(m_ref[...], w_ref[...], _NN)


def _outproj(mm, w, x2, gate, rows_per_mod):
    m, k = mm.shape
    d = w.shape[1]
    tm = _pick(rows_per_mod, 1024, 8)
    per = rows_per_mod // tm
    tn = _pick(d, 512, LANES)
    return pl.pallas_call(
        _outproj_kernel, grid=(m // tm, d // tn),
        in_specs=[pl.BlockSpec((tm, k), lambda i, j: (i, 0)),
                  pl.BlockSpec((k, tn), lambda i, j: (0, j)),
                  pl.BlockSpec((tm, tn), lambda i, j: (i, j)),
                  pl.BlockSpec((None, 1, tn), lambda i, j: (i // per, 0, j))],
        out_specs=pl.BlockSpec((tm, tn), lambda i, j: (i, j)),
        out_shape=jax.ShapeDtypeStruct((m, d), F32),
        compiler_params=_cp("parallel", "parallel"), name="outproj")(mm, w, x2, gate)


def _ffn_up_kernel(x_ref, nw_ref, sh_ref, sc_ref, wg_ref, wu_ref, o_ref, h_scr):
    @pl.when(pl.program_id(1) == 0)
    def _():
        h_scr[...] = _normmod_rows(x_ref[...], nw_ref[...], sh_ref[...], sc_ref[...])

    h = h_scr[...]
    g = _dg(h, wg_ref[...], _NN)
    u = _dg(h, wu_ref[...], _NN)
    o_ref[...] = (_silu(g) * u).astype(o_ref.dtype)


def _ffn_up(x2, nw, sh, sc, wgu, rows_per_mod):
    m, d = x2.shape
    fh = wgu.shape[1] // 2
    tm = _pick(rows_per_mod, 1024, 8)
    per = rows_per_mod // tm
    tn = _pick(fh, 512, LANES)
    nj = fh // tn
    return pl.pallas_call(
        _ffn_up_kernel, grid=(m // tm, nj),
        in_specs=[pl.BlockSpec((tm, d), lambda i, j: (i, 0)),
                  pl.BlockSpec((1, d), lambda i, j: (0, 0)),
                  pl.BlockSpec((None, 1, d), lambda i, j: (i // per, 0, 0)),
                  pl.BlockSpec((None, 1, d), lambda i, j: (i // per, 0, 0)),
                  pl.BlockSpec((d, tn), lambda i, j: (0, j)),
                  pl.BlockSpec((d, tn), lambda i, j: (0, j + nj))],
        out_specs=pl.BlockSpec((tm, tn), lambda i, j: (i, j)),
        out_shape=jax.ShapeDtypeStruct((m, fh), BF16),
        scratch_shapes=[pltpu.VMEM((tm, d), BF16)],
        compiler_params=_cp("parallel", "arbitrary"), name="ffn_up")(x2, nw, sh, sc, wgu, wgu)


def _ffn_down_kernel(a_ref, w_ref, x_ref, g_ref, fw_ref, o_ref):
    kk = pl.program_id(1)

    @pl.when(kk == 0)
    def _():
        o_ref[...] = jnp.zeros_like(o_ref)

    o_ref[...] += _dg(a_ref[...], w_ref[...], _NN)

    @pl.when(kk == pl.num_programs(1) - 1)
    def _():
        x = x_ref[...] + g_ref[...] * o_ref[...]
        y = x * lax.rsqrt(jnp.mean(x * x, axis=-1, keepdims=True) + NORM_EPS)
        o_ref[...] = y * fw_ref[...]


def _ffn_down(act, w, x2, gate, fw, rows_per_mod):
    m, fh = act.shape
    d = w.shape[1]
    tm = _pick(rows_per_mod, 1024, 8)
    per = rows_per_mod // tm
    tk = _pick(fh, 512, LANES)
    return pl.pallas_call(
        _ffn_down_kernel, grid=(m // tm, fh // tk),
        in_specs=[pl.BlockSpec((tm, tk), lambda i, k: (i, k)),
                  pl.BlockSpec((tk, d), lambda i, k: (k, 0)),
                  pl.BlockSpec((tm, d), lambda i, k: (i, 0)),
                  pl.BlockSpec((None, 1, d), lambda i, k: (i // per, 0, 0)),
                  pl.BlockSpec((1, d), lambda i, k: (0, 0))],
        out_specs=pl.BlockSpec((tm, d), lambda i, k: (i, 0)),
        out_shape=jax.ShapeDtypeStruct((m, d), F32),
        compiler_params=_cp("parallel", "arbitrary", vmem=VMEM_LIMIT_BIG), name="ffn_down")(act, w, x2, gate, fw)


def _pad_cols(a, n):
    return jnp.pad(a, ((0, 0), (0, n - a.shape[1])))


def _roundup(n, m):
    return -(-n // m) * m


def kernel(x, c, ctx, c_ctx, w_ada, b_ada, norm1_w, w_in, rw_mu, rw_w0, rw_w2, rw_a0, rw_a2, rw_g2, rw_k_k, rw_k_a, rw_r_k, rw_ln_w, rw_ln_b, gdn_conv_w, gdn_a_log, gdn_dt_bias, gdn_norm_w, merge_p_a, merge_p_b, w_out, norm2_w, ffn_w_gate_up, ffn_w_down, final_norm_w):
    assert w_ada.shape[0] == 1, "single-layer trunk"
    bsz, t_len, d = x.shape
    tc_len = ctx.shape[1]
    h_rw, n_rw = rw_r_k.shape[1:]
    wr = h_rw * n_rw
    rk_w, rk_a, rk_g = rw_w2.shape[2], rw_a2.shape[2], rw_g2.shape[1]
    hg, dg = gdn_a_log.shape[-1], gdn_norm_w.shape[-1]
    wg = hg * dg
    assert n_rw == HALF and dg == LANES and h_rw % 2 == 0
    assert t_len % GRID_W == 0 and t_len % CHUNK == 0 and tc_len % CHUNK == 0

    w_in0 = w_in[0]
    o_wd = 3 * wr
    o_ad = o_wd + 2 * rk_w
    o_gd = o_ad + 2 * rk_a
    rw_cols = o_gd + rk_g
    wdp, adp = _roundup(2 * rk_w, LANES), _roundup(2 * rk_a, LANES)
    unit = 512 if all(n % 512 == 0 for n in (wg, d)) else 2 * LANES
    cpw = _roundup(o_wd + wdp + adp + rk_g, unit)
    gdp = cpw - (o_wd + wdp + adp)

    def rw_layout(a):
        return jnp.concatenate([a[:, :o_wd], _pad_cols(a[:, o_wd:o_ad], wdp), _pad_cols(a[:, o_ad:o_gd], adp),
                                _pad_cols(a[:, o_gd:rw_cols], gdp)], axis=1)

    o_cv = rw_cols
    o_z = o_cv + 3 * wg
    o_ab = o_z + wg
    o_gt = o_ab + 4 * hg
    c_cv = cpw
    c_ab = c_cv + 3 * wg
    c_z = c_ab + unit
    c_gt = c_z + wg
    w_all = jnp.concatenate([rw_layout(w_in0[:, :rw_cols]), w_in0[:, o_cv:o_z], _pad_cols(w_in0[:, o_ab:o_gt], unit),
                             w_in0[:, o_z:o_ab], w_in0[:, o_gt:]], axis=1).astype(BF16)
    mu_p = rw_layout(rw_mu)

    w2f = jnp.zeros((wdp, 2 * wr), F32)
    a2f = jnp.zeros((adp, 2 * wr), F32)
    for dd in range(2):
        w2f = w2f.at[dd * rk_w:(dd + 1) * rk_w, dd * wr:(dd + 1) * wr].set(rw_w2[0, dd])
        a2f = a2f.at[dd * rk_a:(dd + 1) * rk_a, dd * wr:(dd + 1) * wr].set(rw_a2[0, dd])
    gd_w = _roundup(rk_g, LANES)
    c_need = o_wd + wdp + adp + gd_w
    g2p = jnp.pad(rw_g2[0], ((0, gd_w - rk_g), (0, 0)))
    li = jnp.arange(LANES)
    e64 = ((li[:, None] // HALF) == (li[None, :] // HALF)).astype(BF16)
    prm = dict(wr=wr, wdp=wdp, adp=adp, cpw=c_need, mu=mu_p[:, :c_need],
               w0=rw_w0[0].reshape(1, 2 * wr), w2=w2f.astype(BF16),
               a0=rw_a0[0].reshape(1, 2 * wr), a2=a2f.astype(BF16), g2=g2p.astype(BF16),
               k_k=rw_k_k, k_a=rw_k_a, r_k=rw_r_k[0].reshape(1, wr), e64=e64)

    nega = _pad_cols((-jnp.exp(gdn_a_log[0])).reshape(1, 2 * hg), LANES)
    dtb = _pad_cols(gdn_dt_bias[0].reshape(1, 2 * hg), LANES)

    rows = _roundup(bsz + 1, 8)
    cc = jnp.concatenate([c, c_ctx[None, :], jnp.zeros((rows - bsz - 1, d), F32)], axis=0)
    mod = _ada(cc, w_ada[0], b_ada)
    mods = [mod[:bsz, i * d:(i + 1) * d].reshape(bsz, 1, d) for i in range(6)]
    sh1, sc1, gt1, sh2, sc2, gt2 = mods
    csh1 = mod[bsz:bsz + 1, 0:d].reshape(1, 1, d)
    csc1 = mod[bsz:bsz + 1, d:2 * d].reshape(1, 1, d)

    x2 = x.reshape(bsz * t_len, d)
    ctx2 = ctx.reshape(bsz * tc_len, d)
    n_all = w_all.shape[1]
    tn_in = max(t for t in range(unit, INPROJ_TN_MAX + 1, unit) if n_all % t == 0 and c_z % t == 0)
    proj = _inproj(x2, norm1_w, sh1, sc1, w_all, n_all, tn_in, t_len)
    proj_c = _inproj(ctx2, norm1_w, csh1, csc1, w_all, c_z, tn_in, bsz * tc_len)

    def mixer_inputs(p, seq_len, latent):
        feat = _rwfeat(p, prm, bsz, seq_len, latent)
        qkv = _gconv(p, c_cv, 3 * wg, unit, gdn_conv_w[0], bsz, seq_len, hg, dg ** -0.5)
        gb = _gfeat(p, c_ab, nega, dtb, hg)
        nc = seq_len // CHUNK
        grows = gb[:, :4 * hg].reshape(bsz, nc, CHUNK, 4, hg).transpose(0, 4, 1, 3, 2)
        return feat, qkv, grows

    feat_c, qkv_c, grows_c = mixer_inputs(proj_c, tc_len, False)
    feat_l, qkv_l, grows_l = mixer_inputs(proj, t_len, True)

    def run_mixer(prep_c, prep_l, nseq):
        s0 = jnp.zeros((2, nseq, LANES, LANES), F32)
        (s_ctx,) = _scan(prep_c[:3], prep_c[3:], s0, False)
        o_f, o_r, _ = _scan(prep_l[:5], prep_l[5:], s_ctx, True)
        return o_f, o_r

    o_f, o_r = run_mixer(_rwprep(feat_c, bsz, tc_len, False), _rwprep(feat_l, bsz, t_len, True),
                         bsz * (wr // LANES))
    o_a = _rw_readout(o_f, o_r, feat_l[7], feat_l[6], rw_ln_w, rw_ln_b, e64, bsz)

    o_f, o_r = run_mixer(_gprep(qkv_c, grows_c, bsz, tc_len, hg, False),
                         _gprep(qkv_l, grows_l, bsz, t_len, hg, True), bsz * hg)
    o_b = _g_readout(o_f, o_r, proj, c_z, gdn_norm_w, bsz)

    mm = _merge(o_a, o_b, merge_p_a[0].astype(BF16), merge_p_b[0].astype(BF16), proj, c_gt, unit)
    x1 = _outproj(mm, w_out[0].astype(BF16), x2, gt1, t_len)
    act = _ffn_up(x1, norm2_w, sh2, sc2, ffn_w_gate_up[0].astype(BF16), t_len)
    out = _ffn_down(act, ffn_w_down[0].astype(BF16), x1, gt2, final_norm_w.reshape(1, d), t_len)
    return out.reshape(bsz, t_len, d)
```

```python
import functools
import math

import jax
import jax.numpy as jnp
from jax import lax
from jax.experimental import pallas as pl
from jax.experimental.pallas import tpu as pltpu

F32 = jnp.float32
BF16 = jnp.bfloat16

GRID_W = 64
NORM_EPS = 1e-6
RW_GN_EPS = 64e-5
CHUNK = 64
LANES = 128
HALF = 64
VMEM_LIMIT = 48 * 1024 * 1024
VMEM_LIMIT_BIG = 58 * 1024 * 1024


def _pick(n, pref, mult):
    t = min(pref, n)
    t -= t % mult
    while t > mult and n % t:
        t -= mult
    assert t >= mult and n % t == 0, (n, pref, mult)
    return t


def _cp(*sem, vmem=VMEM_LIMIT):
    return pltpu.CompilerParams(dimension_semantics=sem, vmem_limit_bytes=vmem)


def _sigmoid(x):
    return 1.0 / (1.0 + jnp.exp(-x))


def _silu(x):
    return x * _sigmoid(x)


def _softplus(x):
    return jnp.maximum(x, 0.0) + jnp.log(1.0 + jnp.exp(-jnp.abs(x)))


def _split(x):
    hi = x.astype(BF16)
    lo = (x - hi.astype(F32)).astype(BF16)
    return hi, lo


_NN = (((1,), (0,)), ((), ()))
_NT = (((1,), (1,)), ((), ()))
_TN = (((0,), (0,)), ((), ()))


def _dg(a, b, dn):
    return lax.dot_general(a, b, dn, preferred_element_type=F32)


def _mm(a, b, dn=_NN, passes=1):
    if passes == 1:
        return _dg(a.astype(BF16), b.astype(BF16), dn)
    ah, al = _split(a)
    bh, bl = _split(b)
    return _dg(ah, bh, dn) + (_dg(ah, bl, dn) + _dg(al, bh, dn))


def _mm_exact_lhs(a_bf, b, dn=_NN):
    bh, bl = _split(b)
    return _dg(a_bf, bh, dn) + _dg(a_bf, bl, dn)


def _iota(shape, dim):
    return lax.broadcasted_iota(jnp.int32, shape, dim)


def _ada_kernel(c_ref, w_ref, b_ref, o_ref):
    s = _silu(c_ref[...])
    o_ref[...] = _dg(s.astype(BF16), w_ref[...].astype(BF16), _NN) + b_ref[...]


def _ada(cc, w, b):
    m, d = cc.shape
    n = w.shape[1]
    tn = _pick(n, 1024, LANES)
    return pl.pallas_call(
        _ada_kernel, grid=(n // tn,),
        in_specs=[pl.BlockSpec((m, d), lambda j: (0, 0)),
                  pl.BlockSpec((d, tn), lambda j: (0, j)),
                  pl.BlockSpec((1, tn), lambda j: (0, j))],
        out_specs=pl.BlockSpec((m, tn), lambda j: (0, j)),
        out_shape=jax.ShapeDtypeStruct((m, n), F32),
        compiler_params=_cp("parallel"), name="ada")(cc, w, b)


def _normmod_rows(x, nw, sh, sc):
    y = x * lax.rsqrt(jnp.mean(x * x, axis=-1, keepdims=True) + NORM_EPS) * nw
    return (y * (1.0 + sc) + sh).astype(BF16)


INPROJ_TN_MAX = 1280


def _inproj_kernel(x_ref, nw_ref, sh_ref, sc_ref, w_ref, o_ref, h_scr):
    @pl.when(pl.program_id(1) == 0)
    def _():
        h_scr[...] = _normmod_rows(x_ref[...], nw_ref[...], sh_ref[...], sc_ref[...])

    o_ref[...] = _dg(h_scr[...], w_ref[...], _NN).astype(o_ref.dtype)


def _inproj(x2, nw, sh, sc, w, ncols, tn, rows_per_mod):
    m, d = x2.shape
    tm = _pick(rows_per_mod, 1024, 8)
    per = rows_per_mod // tm
    return pl.pallas_call(
        _inproj_kernel, grid=(m // tm, ncols // tn),
        in_specs=[pl.BlockSpec((tm, d), lambda i, j: (i, 0)),
                  pl.BlockSpec((1, d), lambda i, j: (0, 0)),
                  pl.BlockSpec((None, 1, d), lambda i, j: (i // per, 0, 0)),
                  pl.BlockSpec((None, 1, d), lambda i, j: (i // per, 0, 0)),
                  pl.BlockSpec((d, tn), lambda i, j: (0, j))],
        out_specs=pl.BlockSpec((tm, tn), lambda i, j: (i, j)),
        out_shape=jax.ShapeDtypeStruct((m, ncols), BF16),
        scratch_shapes=[pltpu.VMEM((tm, d), BF16)],
        compiler_params=_cp("parallel", "arbitrary"), name="inproj")(x2, nw, sh, sc, w)


def _shiftmix(mode, t_len, i, p_bf, prev, nxt_rows, mu):
    p = p_bf.astype(F32)
    shp = p.shape
    tt = shp[0]
    t = i * tt + _iota(shp, 0)
    lane = _iota(shp, 1)
    rr = _iota((tt, tt), 0)
    cc = _iota((tt, tt), 1)
    tg = i * tt + rr
    if mode == "grid":
        ok_l, ok_r = (tg & (GRID_W - 1)) != 0, (tg & (GRID_W - 1)) != GRID_W - 1
    else:
        ok_l, ok_r = tg != 0, tg != t_len - 1
    s_l = jnp.where(cc == rr - 1, jnp.where(ok_l, 1.0, 0.0), 0.0).astype(BF16)
    s_r = jnp.where(cc == rr + 1, jnp.where(ok_r, 1.0, 0.0), 0.0).astype(BF16)
    left = _dg(s_l, p_bf, _NN)
    right = _dg(s_r, p_bf, _NN)
    if mode == "grid":
        cl = lane & 3
        if tt > GRID_W:
            up = jnp.concatenate([prev, p[:tt - GRID_W]], axis=0)
            down = jnp.concatenate([p[GRID_W:], nxt_rows], axis=0)
        else:
            up, down = prev, nxt_rows
        up = jnp.where(t >= GRID_W, up, 0.0)
        down = jnp.where(t < t_len - GRID_W, down, 0.0)
        sh = jnp.where(cl == 0, left, jnp.where(cl == 1, right, jnp.where(cl == 2, up, down)))
    else:
        sh = jnp.where((lane & 1) == 0, left, right)
    return p + (sh - p) * mu


def _segsum(x, e):
    outs = []
    for g in range(x.shape[1] // LANES):
        outs.append(_mm_exact_lhs_rhs(x[:, g * LANES:(g + 1) * LANES], e))
    return outs[0] if len(outs) == 1 else jnp.concatenate(outs, axis=1)


def _mm_exact_lhs_rhs(x, e_bf):
    hi, lo = _split(x)
    return _dg(hi, e_bf, _NN) + _dg(lo, e_bf, _NN)


def _rwfeat_kernel(wr, wdp, adp, latent, t_len, p_ref, prev_ref, next_ref, mu_ref, w0_ref, w2_ref, a0_ref, a2_ref,
                   g2_ref, kk_ref_, ka_ref, rk_ref, e_ref, *outs):
    if latent:
        r_o, v_o, kk_o, lw_o, kd_o, bd_o, gate_o, bonus_o = outs
    else:
        r_o, v_o, kk_o, lw_o, kd_o, bd_o = outs
    e = e_ref[...]
    xs = _shiftmix("grid" if latent else "seq", t_len, pl.program_id(1), p_ref[...],
                   prev_ref[...].astype(F32), next_ref[...].astype(F32), mu_ref[...])
    r = xs[:, 0:wr]
    k = xs[:, wr:2 * wr]
    v = xs[:, 2 * wr:3 * wr]
    wd = xs[:, 3 * wr:3 * wr + wdp]
    ad = xs[:, 3 * wr + wdp:3 * wr + wdp + adp]
    lw = _dg(jnp.tanh(wd).astype(BF16), w2_ref[...], _NN) + w0_ref[...]
    w_log = -_softplus(-lw) - 0.5
    logw = -jnp.exp(w_log)
    iclr = _sigmoid(_dg(ad.astype(BF16), a2_ref[...], _NN) + a0_ref[...])
    kk0 = k * kk_ref_[...]
    kk = kk0 * lax.rsqrt(_segsum(kk0 * kk0, e) + NORM_EPS)
    r_o[...] = r.astype(BF16)
    v_o[...] = v.astype(BF16)
    kk_o[...] = kk.astype(BF16)
    ka = ka_ref[...]
    kdirs = []
    for d in range(2):
        ic = iclr[:, d * wr:(d + 1) * wr]
        kd = k * (1.0 + (ic - 1.0) * ka)
        kdirs.append(kd)
        lw_o[d] = logw[:, d * wr:(d + 1) * wr]
        kd_o[d] = kd.astype(BF16)
        bd_o[d] = (kk * ic).astype(BF16)
    if latent:
        gd = xs[:, 3 * wr + wdp + adp:]
        gate_o[...] = _dg(_sigmoid(gd).astype(BF16), g2_ref[...], _NN).astype(BF16)
        kb = 0.5 * (kdirs[0] + kdirs[1])
        bonus_o[...] = (_segsum(r * kb * rk_ref[...], e) * v).astype(BF16)


def _rwfeat(p, prm, bsz, t_len, latent):
    m = bsz * t_len
    wr, wdp, adp, cpw = prm["wr"], prm["wdp"], prm["adp"], prm["cpw"]
    tm = _pick(t_len, 256, GRID_W) if latent else t_len
    nt = t_len // tm
    hb = tm // GRID_W
    nh = m // GRID_W
    full = lambda b, i: (0, 0)
    names = ("mu", "w0", "w2", "a0", "a2", "g2", "k_k", "k_a", "r_k", "e64")
    w_specs = [pl.BlockSpec(prm[n].shape, full) for n in names]
    tok = pl.BlockSpec((tm, wr), lambda b, i: (b * nt + i, 0))
    tok2 = pl.BlockSpec((2, tm, wr), lambda b, i: (0, b * nt + i, 0))
    s1 = jax.ShapeDtypeStruct((m, wr), BF16)
    s2 = jax.ShapeDtypeStruct((2, m, wr), BF16)
    out_specs = [tok, tok, tok, tok2, tok2, tok2]
    out_shape = [s1, s1, s1, jax.ShapeDtypeStruct((2, m, wr), F32), s2, s2]
    if latent:
        out_specs += [tok, tok]
        out_shape += [s1, s1]
    kern = functools.partial(_rwfeat_kernel, wr, wdp, adp, latent, t_len)
    prev = pl.BlockSpec((GRID_W, cpw), lambda b, i: (jnp.maximum((b * nt + i) * hb - 1, 0), 0))
    nxt = pl.BlockSpec((GRID_W, cpw), lambda b, i: (jnp.minimum((b * nt + i + 1) * hb, nh - 1), 0))
    return pl.pallas_call(
        kern, grid=(bsz, nt),
        in_specs=[pl.BlockSpec((tm, cpw), lambda b, i: (b * nt + i, 0)), prev, nxt] + w_specs,
        out_specs=out_specs, out_shape=out_shape,
        compiler_params=_cp("parallel", "parallel"), name="rwfeat")(p, p, p, *[prm[n] for n in names])


INV_PASSES = 1
PREP_PASSES = 1
PREP_CHAINS = 16


def _prep_unroll(tc, chains_per_chunk):
    u = PREP_CHAINS // chains_per_chunk
    while tc % u:
        u //= 2
    return u


def _stack(x, h0):
    return jnp.concatenate([jnp.where(h0, x, 0.0), jnp.where(h0, 0.0, x)], axis=0)


def _tri_inverse_many(mats, eye2, h0):
    c = CHUNK
    steps = int(math.log2(c)) - 1
    ps = [eye2 - a for a in mats]
    xs = [_mm(a, _stack(a, h0), passes=INV_PASSES) for a in mats]
    for i in range(steps):
        last = i == steps - 1
        for n in range(len(mats)):
            rhs = _stack(xs[n], h0)
            if last:
                ps[n] = ps[n] + _mm(ps[n], rhs, passes=INV_PASSES)
            else:
                both = _mm(jnp.concatenate([ps[n], xs[n]], axis=0), rhs, passes=INV_PASSES)
                ps[n] = ps[n] + both[:c]
                xs[n] = both[c:]
    return ps


def _cumsum_rows(x, rev, row):
    n = x.shape[0]
    k = 1
    while k < n:
        if rev:
            x = x + jnp.where(row < n - k, pltpu.roll(x, n - k, 0), 0.0)
        else:
            x = x + jnp.where(row >= k, pltpu.roll(x, k, 0), 0.0)
        k *= 2
    return x


def _rwprep_chunks(need_out, insts):
    c = CHUNK
    shp = (c, LANES)
    row = _iota(shp, 0)
    lane = _iota(shp, 1)
    col = lane & (HALF - 1)
    h0 = lane < HALF
    eye2 = jnp.where(col == row, 1.0, 0.0)
    bdm = (_iota((LANES, LANES), 0) < HALF) == (_iota((LANES, LANES), 1) < HALF)
    strict ={False: col < row, True: col > row}
    incl = {False: col <= row, True: col >= row}
    n = len(insts)
    gs = [_cumsum_rows(lw, rev, row) for (rev, r, v, kk, lw, k, bb) in insts]
    st = []
    for (rev, r, v, kk, lw, k, bb), g in zip(insts, gs):
        last, mid = (0, c // 2) if rev else (c - 1, c // 2 - 1)
        gm = g - lw
        gtot = g[last:last + 1, :]
        rho = g[mid:mid + 1, :]
        e_inv = jnp.exp(rho - g)
        kkd = kk * jnp.exp(gm - rho)
        et = jnp.exp(gtot - g)
        d = dict(rev=rev, r=r, v=v, g=g, kt=k * et, bbt=bb * et, pc=jnp.exp(gtot), kkd0=kk * jnp.exp(gm))
        lhs = jnp.concatenate([kkd, r * jnp.exp(g - rho)], axis=0) if need_out else kkd
        rhs = jnp.concatenate([_stack(bb * e_inv, h0), _stack(k * e_inv, h0)], axis=0)
        d["ab"] = _mm(lhs, rhs, _NT, PREP_PASSES)
        st.append(d)
    for d in st:
        ab = d["ab"]
        d["a"] = jnp.where(strict[d["rev"]], ab[:c, :LANES], 0.0)
        bm = jnp.where(strict[d["rev"]], ab[:c, LANES:], 0.0)
        if need_out:
            bbr = _mm(jnp.concatenate([bm, jnp.where(incl[d["rev"]], ab[c:, LANES:], 0.0)], axis=0),
                      _stack(d["v"], h0), passes=PREP_PASSES)
            d["bv"], d["brv"] = bbr[:c], bbr[c:]
        else:
            d["bv"] = _mm(bm, _stack(d["v"], h0), passes=PREP_PASSES)
    ts = _tri_inverse_many([d["a"] for d in st], eye2, h0)
    for d, t in zip(st, ts):
        wu = _mm(t, jnp.concatenate([_stack(d["kkd0"], h0), _stack(d["bv"], h0)], axis=1), passes=PREP_PASSES)
        d["w"], d["u0"] = wu[:, :LANES], wu[:, LANES:]
    out = []
    for d in st:
        w, u0, bbt = d["w"], d["u0"], d["bbt"]
        mn = jnp.where(bdm, _mm(w, bbt, _TN, PREP_PASSES), 0.0)
        nnt = jnp.where(bdm, _mm(jnp.concatenate([d["v"], u0], axis=0), jnp.concatenate([d["kt"], -bbt], axis=0),
                                 _TN, PREP_PASSES), 0.0)
        if not need_out:
            out.append((None, None, mn, nnt, d["pc"]))
            continue
        ar = jnp.where(incl[d["rev"]], d["ab"][c:, :LANES], 0.0)
        arwu = _mm(ar, jnp.concatenate([_stack(w, h0), _stack(u0, h0)], axis=1), passes=PREP_PASSES)
        qp = d["r"] * jnp.exp(d["g"]) - arwu[:, :LANES]
        o0 = d["brv"] - arwu[:, LANES:]
        out.append((qp, o0, mn, nnt, d["pc"]))
    return out


def _store_prep(o, need_out, g, ci, sl, res):
    qp, o0, mn, nnt, pc = res
    if need_out:
        o[0][g, sl, :] = qp.astype(BF16)
        o[1][g, sl, :] = o0.astype(BF16)
        o = o[2:]
    o[0][g, ci] = mn.astype(BF16)
    o[1][g, ci] = nnt.astype(BF16)
    o[2][g, pl.ds(ci, 1), :] = pc


def _rwprep_kernel(tc, ng, need_out, r_ref, v_ref, kk_ref, lwf_ref, lwr_ref, kf_ref, kr_ref, bf_ref, br_ref, *outs):
    per = 5 if need_out else 3
    outs_d = (outs[:per], outs[per:])
    ins_d = ((lwf_ref, kf_ref, bf_ref), (lwr_ref, kr_ref, br_ref))

    unroll = _prep_unroll(tc, 2 * ng)

    def body(it, carry):
        insts, where = [], []
        for g in range(ng):
            cs = slice(g * LANES, (g + 1) * LANES)
            for u in range(unroll):
                ci = it * unroll + u
                sl = pl.ds(pl.multiple_of(ci * CHUNK, CHUNK), CHUNK)
                r, v, kk = (ref[sl, cs].astype(F32) for ref in (r_ref, v_ref, kk_ref))
                for d in range(2):
                    lw_ref, k_ref, b_ref = ins_d[d]
                    insts.append((d == 1, r, v, kk, lw_ref[sl, cs], k_ref[sl, cs].astype(F32),
                                  b_ref[sl, cs].astype(F32)))
                    where.append((d, g, ci, sl))
        for (d, g, ci, sl), res in zip(where, _rwprep_chunks(need_out, insts)):
            _store_prep(outs_d[d], need_out, g, ci, sl, res)
        return carry

    lax.fori_loop(0, tc // unroll, body, 0)


def _prep_out(nseq, t_len, need_out):
    nc = t_len // CHUNK
    shapes = []
    if need_out:
        shapes += [jax.ShapeDtypeStruct((nseq, t_len, LANES), BF16)] * 2
    shapes += [jax.ShapeDtypeStruct((nseq, nc, LANES, LANES), BF16)] * 2
    shapes += [jax.ShapeDtypeStruct((nseq, nc, LANES), F32)]
    return shapes


def _prep_out_specs(tc, ng, need_out, seq_blk):
    specs = []
    if need_out:
        specs += [pl.BlockSpec((ng, tc * CHUNK, LANES), lambda b, p, j: (seq_blk(b, p), j, 0))] * 2
    specs += [pl.BlockSpec((ng, tc, LANES, LANES), lambda b, p, j: (seq_blk(b, p), j, 0, 0))] * 2
    specs += [pl.BlockSpec((ng, tc, LANES), lambda b, p, j: (seq_blk(b, p), j, 0))]
    return specs


def _chunks_per_step(nc, pref=8):
    return pref if nc % pref == 0 else (8 if nc % 8 == 0 else nc)


def _seqs_per_step(nseq_per_batch, tc, chains_per_chunk):
    return _pick(nseq_per_batch, max(PREP_CHAINS // (tc * chains_per_chunk), 1), 1)


def _rwprep(feat, bsz, t_len, need_out):
    r, v, kk, lw, kd, bd = feat[:6]
    wr = r.shape[1]
    npair = wr // LANES
    nc = t_len // CHUNK
    tc = _chunks_per_step(nc)
    nblk = nc // tc
    ng = _seqs_per_step(npair, tc, 2)
    npg = npair // ng
    tok = pl.BlockSpec((tc * CHUNK, ng * LANES), lambda b, p, j: (b * nblk + j, p))
    tok_d = lambda d: pl.BlockSpec((None, tc * CHUNK, ng * LANES), lambda b, p, j: (d, b * nblk + j, p))
    seq_of = lambda b, p: b * npg + p
    kern = functools.partial(_rwprep_kernel, tc, ng, need_out)
    return pl.pallas_call(
        kern, grid=(bsz, npg, nblk),
        in_specs=[tok, tok, tok, tok_d(0), tok_d(1), tok_d(0), tok_d(1), tok_d(0), tok_d(1)],
        out_specs=_prep_out_specs(tc, ng, need_out, seq_of) * 2,
        out_shape=_prep_out(bsz * npair, t_len, need_out) * 2,
        compiler_params=_cp("parallel", "parallel", "parallel"), name="rwprep")(
            r, v, kk, lw, lw, kd, kd, bd, bd)


SCAN_GROUP = 8
SCAN_CHUNKS = 8
SCAN_UNROLL = 4


def _scan_kernel(tc, ng, need_out, *refs):
    per = 5 if need_out else 3
    ins = (refs[:per], refs[per:2 * per])
    s0_ref = refs[2 * per]
    rest = refs[2 * per + 1:]
    if need_out:
        o_refs = rest[:2]
        rest = rest[2:]
    sfin_ref, s_scr = rest
    j = pl.program_id(1)

    @pl.when(j == 0)
    def _():
        s_scr[...] = s0_ref[...]

    unroll = _pick(tc, SCAN_UNROLL, 1)

    def body(it, carry):
        states = [[s_scr[d, g] for d in range(2)] for g in range(ng)]
        chains = [(g, d) for g in range(ng) for d in range(2)]
        st_refs = tuple(r[2:] if need_out else r for r in ins)
        for u in range(unroll):
            i = it * unroll + u
            cis = (i, tc - 1 - i)
            s_bf = {c: states[c[0]][c[1]].astype(BF16) for c in chains}
            upd = {(g, d): _dg(s_bf[(g, d)], st_refs[d][0][g, cis[d]], _NN) for (g, d) in chains}
            if need_out:
                for (g, d) in chains:
                    sl = pl.ds(pl.multiple_of(cis[d] * CHUNK, CHUNK), CHUNK)
                    o = _dg(ins[d][0][g, sl, :], s_bf[(g, d)], _NT) + ins[d][1][g, sl, :].astype(F32)
                    o_refs[d][g, sl, :] = o.astype(o_refs[d].dtype)
            for (g, d) in chains:
                states[g][d] = (states[g][d] * st_refs[d][2][g, cis[d]] - upd[(g, d)]
                                + st_refs[d][1][g, cis[d]].astype(F32))
        for g in range(ng):
            for d in range(2):
                s_scr[d, g] = states[g][d]
        return carry

    lax.fori_loop(0, tc // unroll, body, 0)

    @pl.when(j == pl.num_programs(1) - 1)
    def _():
        sfin_ref[...] = s_scr[...]


def _scan(prep_f, prep_r, s0, need_out):
    mn = prep_f[-3]
    nseq, nc = mn.shape[0], mn.shape[1]
    tc = _pick(nc, SCAN_CHUNKS, 1)
    nblk = nc // tc
    ng = _pick(nseq, SCAN_GROUP, 1)
    prep_f = list(prep_f[:-1]) + [prep_f[-1].reshape(nseq, nc, 1, LANES)]
    prep_r = list(prep_r[:-1]) + [prep_r[-1].reshape(nseq, nc, 1, LANES)]

    def specs(rev):
        jj = (lambda j: nblk - 1 - j) if rev else (lambda j: j)
        sp = []
        if need_out:
            sp += [pl.BlockSpec((ng, tc * CHUNK, LANES), lambda s, j: (s, jj(j), 0))] * 2
        sp += [pl.BlockSpec((ng, tc, LANES, LANES), lambda s, j: (s, jj(j), 0, 0))] * 2
        sp += [pl.BlockSpec((ng, tc, 1, LANES), lambda s, j: (s, jj(j), 0, 0))]
        return sp

    st_spec = pl.BlockSpec((2, ng, LANES, LANES), lambda s, j: (0, s, 0, 0))
    out_specs, out_shape = [], []
    if need_out:
        out_specs += [specs(False)[0], specs(True)[0]]
        out_shape += [jax.ShapeDtypeStruct((nseq, nc * CHUNK, LANES), BF16)] * 2
    out_specs.append(st_spec)
    out_shape.append(jax.ShapeDtypeStruct((2, nseq, LANES, LANES), F32))
    kern = functools.partial(_scan_kernel, tc, ng, need_out)
    return pl.pallas_call(
        kern, grid=(nseq // ng, nblk),
        in_specs=specs(False) + specs(True) + [st_spec],
        out_specs=out_specs, out_shape=out_shape,
        scratch_shapes=[pltpu.VMEM((2, ng, LANES, LANES), F32)],
        compiler_params=_cp("parallel", "arbitrary"), name="scan")(*prep_f, *prep_r, s0)


READOUT_GROUP = 4


def _rw_readout_kernel(of_ref, or_ref, bonus_ref, gate_ref, lnw_ref, lnb_ref, e_ref, o_ref):
    e = e_ref[...]
    inv = 1.0 / HALF
    for g in range(of_ref.shape[0]):
        cs = slice(g * LANES, (g + 1) * LANES)
        y = of_ref[g].astype(F32) + or_ref[g].astype(F32)
        mu = _segsum(y, e) * inv
        yc = y - mu
        var = _segsum(yc * yc, e) * inv
        yn = yc * lax.rsqrt(var + RW_GN_EPS) * lnw_ref[:, cs] + lnb_ref[:, cs]
        o_ref[:, cs] = ((yn + bonus_ref[:, cs].astype(F32)) * gate_ref[:, cs].astype(F32)).astype(o_ref.dtype)


def _rw_readout(o_f, o_r, bonus, gate, lnw, lnb, e64, bsz):
    nseq, t_len, _ = o_f.shape
    npair = nseq // bsz
    ng = _pick(npair, READOUT_GROUP, 1)
    npg = npair // ng
    tt = _pick(t_len, 512, 8)
    nt = t_len // tt
    seq = pl.BlockSpec((ng, tt, LANES), lambda b, p, i: (b * npg + p, i, 0))
    tok = pl.BlockSpec((tt, ng * LANES), lambda b, p, i: (b * nt + i, p))
    par = pl.BlockSpec((1, ng * LANES), lambda b, p, i: (0, p))
    return pl.pallas_call(
        _rw_readout_kernel, grid=(bsz, npg, nt),
        in_specs=[seq, seq, tok, tok, par, par, pl.BlockSpec((LANES, LANES), lambda b, p, i: (0, 0))],
        out_specs=tok, out_shape=jax.ShapeDtypeStruct(bonus.shape, BF16),
        compiler_params=_cp("parallel", "parallel", "parallel"), name="rw_readout")(
            o_f, o_r, bonus, gate, lnw, lnb, e64)


CONV_HALO = 16


def _gconv_kernel(taps, nq, scale, p_ref, prev_ref, next_ref, w_ref, o_ref):
    i = pl.program_id(1)
    cb = pl.program_id(2)
    main = p_ref[...].astype(F32)
    tt = main.shape[0]
    prev = jnp.where(i > 0, prev_ref[...].astype(F32), 0.0)
    nxt = jnp.where(i < pl.num_programs(1) - 1, next_ref[...].astype(F32), 0.0)
    xcat = jnp.concatenate([prev, main, nxt], axis=0)
    n = tt + 2 * CONV_HALO
    half = taps // 2
    acc = None
    for j in range(taps):
        sh = (half - j) % n
        xs = xcat if sh == 0 else pltpu.roll(xcat, sh, 0)
        term = xs[CONV_HALO:CONV_HALO + tt] * w_ref[j:j + 1, :]
        acc = term if acc is None else acc + term
    u = _silu(acc)
    heads = u.shape[1] // LANES
    for s in range(heads):
        hd = cb * heads + s
        us = u[:, s * LANES:(s + 1) * LANES]
        un = us * lax.rsqrt(jnp.sum(us * us, axis=-1, keepdims=True) + NORM_EPS)
        un = un * jnp.where(hd < nq, scale, 1.0)
        o_ref[:, s * LANES:(s + 1) * LANES] = jnp.where(hd < 2 * nq, un, us).astype(o_ref.dtype)


def _gconv(p, col0, c3, cb, conv_w, bsz, t_len, nq, scale):
    m = bsz * t_len
    taps = conv_w.shape[0]
    assert taps // 2 <= CONV_HALO and col0 % cb == 0 and c3 % cb == 0
    tt = _pick(t_len, 512, CONV_HALO)
    nt = t_len // tt
    hb = tt // CONV_HALO
    nh = m // CONV_HALO
    c0 = col0 // cb
    kern = functools.partial(_gconv_kernel, taps, nq, scale)
    return pl.pallas_call(
        kern, grid=(bsz, nt, c3 // cb),
        in_specs=[pl.BlockSpec((tt, cb), lambda b, i, c: (b * nt + i, c0 + c)),
                  pl.BlockSpec((CONV_HALO, cb), lambda b, i, c: (jnp.maximum((b * nt + i) * hb - 1, 0), c0 + c)),
                  pl.BlockSpec((CONV_HALO, cb),
                               lambda b, i, c: (jnp.minimum((b * nt + i + 1) * hb, nh - 1), c0 + c)),
                  pl.BlockSpec((taps, cb), lambda b, i, c: (0, c))],
        out_specs=pl.BlockSpec((tt, cb), lambda b, i, c: (b * nt + i, c)),
        out_shape=jax.ShapeDtypeStruct((m, c3), BF16),
        compiler_params=_cp("parallel", "parallel", "parallel"), name="gconv")(p, p, p, conv_w)


def _gfeat_kernel(hg, ab_ref, nega_ref, dtb_ref, o_ref):
    ab = ab_ref[...].astype(F32)
    tm = ab.shape[0]
    lane = _iota(ab.shape, 1)
    g = jnp.where(lane < 2 * hg, nega_ref[...] * _softplus(ab + dtb_ref[...]), 0.0)
    beta = _sigmoid(ab)
    rr = _iota((tm, tm), 0)
    cc = _iota((tm, tm), 1)
    same = (rr // CHUNK) == (cc // CHUNK)
    lf = jnp.where(same & (cc <= rr), 1.0, 0.0).astype(BF16)
    lr = jnp.where(same & (cc >= rr), 1.0, 0.0).astype(BF16)
    gcf = _mm_exact_lhs(lf, g)
    gcr = _mm_exact_lhs(lr, g)
    gc = jnp.where(lane < hg, gcf, gcr)
    o_ref[...] = jnp.where(lane < 2 * hg, gc, beta)


def _gfeat(p, col0, nega, dtb, hg):
    m = p.shape[0]
    tm = _pick(m, 256, CHUNK)
    c0 = col0 // LANES
    return pl.pallas_call(
        functools.partial(_gfeat_kernel, hg), grid=(m // tm,),
        in_specs=[pl.BlockSpec((tm, LANES), lambda i: (i, c0)),
                  pl.BlockSpec((1, LANES), lambda i: (0, 0)),
                  pl.BlockSpec((1, LANES), lambda i: (0, 0))],
        out_specs=pl.BlockSpec((tm, LANES), lambda i: (i, 0)),
        out_shape=jax.ShapeDtypeStruct((m, LANES), F32),
        compiler_params=_cp("parallel"), name="gfeat")(p, nega, dtb)


def _col_from_row(rowv, eye):
    c = rowv.shape[1]
    return jnp.sum(jnp.where(eye, jnp.broadcast_to(rowv, (c, c)), 0.0), axis=1, keepdims=True)


def _gprep_chunks(need_out, insts):
    c = CHUNK
    shp = (c, LANES)
    row = _iota(shp, 0)
    lane = _iota(shp, 1)
    col = lane & (HALF - 1)
    f0 = lane < HALF
    eye = _iota((c, c), 0) == _iota((c, c), 1)
    ahead = jnp.where(f0, row - col, col - row)
    strict = ahead > 0
    incl = ahead >= 0
    eye2 = jnp.where(col == row, 1.0, 0.0)
    st = []
    for (q, k, v, rows) in insts:
        kk2 = jnp.concatenate([k, k], axis=0)
        lhs = jnp.concatenate([k, q], axis=0) if need_out else k
        st.append(dict(q=q, k=k, v=v, rows=rows, kq=_mm(lhs, kk2, _NT, PREP_PASSES)))
    for d in st:
        rows = d["rows"]
        gr = (rows[0:1, :], rows[1:2, :])
        gcol = [_col_from_row(x, eye) for x in gr]
        bcol = [_col_from_row(rows[2 + dd:3 + dd, :], eye) for dd in range(2)]
        grow2 = jnp.concatenate(gr, axis=1)
        gcol2 = jnp.where(f0, gcol[0], gcol[1])
        bcol2 = jnp.where(f0, bcol[0], bcol[1])
        decay = jnp.where(incl, jnp.exp(jnp.where(incl, gcol2 - grow2, 0.0)), 0.0)
        d.update(gcol=gcol, bcol=bcol, decay=decay, glast=(gr[0][:, c - 1:c], gr[1][:, 0:1]),
                 lmat=jnp.where(strict, d["kq"][:c] * bcol2 * decay, 0.0))
    ts = _tri_inverse_many([d["lmat"] for d in st], eye2, f0)
    for d, t in zip(st, ts):
        k, v, gcol, bcol = d["k"], d["v"], d["gcol"], d["bcol"]
        egc = [jnp.exp(gcol[dd]) for dd in range(2)]
        rhs2 = jnp.concatenate([jnp.concatenate([v * bcol[dd], k * (bcol[dd] * egc[dd])], axis=1)
                                for dd in range(2)], axis=0)
        d["egc"] = egc
        both = _mm(_stack(t, f0), rhs2, passes=PREP_PASSES)
        d["sol"] = [both[:c], both[c:]]
    res = []
    for d in st:
        sol, k, q = d["sol"], d["k"], d["q"]
        if need_out:
            qk = jnp.where(incl, d["kq"][c:] * d["decay"], 0.0)
            both = _mm(_stack(qk, f0), jnp.concatenate(sol, axis=0), passes=PREP_PASSES)
            xx = [both[:c], both[c:]]
        out = []
        for dd in range(2):
            u = sol[dd][:, :LANES]
            wk = sol[dd][:, LANES:]
            ktail = k * jnp.exp(d["glast"][dd] - d["gcol"][dd])
            mn = _mm(wk, ktail, _TN, PREP_PASSES)
            nnt = _mm(u, ktail, _TN, PREP_PASSES)
            pc = jnp.broadcast_to(jnp.exp(d["glast"][dd]), (1, LANES))
            if need_out:
                out.append((q * d["egc"][dd] - xx[dd][:, LANES:], xx[dd][:, :LANES], mn, nnt, pc))
            else:
                out.append((None, None, mn, nnt, pc))
        res.append(out)
    return res


def _gprep_kernel(tc, ng, need_out, q_ref, k_ref, v_ref, rows_ref, *outs):
    per = 5 if need_out else 3
    outs_d = (outs[:per], outs[per:])

    unroll = _prep_unroll(tc, ng)

    def body(it, carry):
        insts, where = [], []
        for g in range(ng):
            cs = slice(g * LANES, (g + 1) * LANES)
            for u in range(unroll):
                ci = it * unroll + u
                sl = pl.ds(pl.multiple_of(ci * CHUNK, CHUNK), CHUNK)
                q = q_ref[sl, cs].astype(F32) if need_out else None
                insts.append((q, k_ref[sl, cs].astype(F32), v_ref[sl, cs].astype(F32), rows_ref[g, ci]))
                where.append((g, ci, sl))
        for (g, ci, sl), res in zip(where, _gprep_chunks(need_out, insts)):
            for d in range(2):
                _store_prep(outs_d[d], need_out, g, ci, sl, res[d])
        return carry

    lax.fori_loop(0, tc // unroll, body, 0)


def _gprep(qkv, rows, bsz, t_len, hg, need_out):
    nc = t_len // CHUNK
    tc = _chunks_per_step(nc, PREP_CHAINS)
    nblk = nc // tc
    ng = _seqs_per_step(hg, tc, 1)
    npg = hg // ng
    tok = lambda off: pl.BlockSpec((tc * CHUNK, ng * LANES), lambda b, h, j: (b * nblk + j, off * npg + h))
    seq_of = lambda b, h: b * npg + h
    kern = functools.partial(_gprep_kernel, tc, ng, need_out)
    return pl.pallas_call(
        kern, grid=(bsz, npg, nblk),
        in_specs=[tok(0), tok(1), tok(2),
                  pl.BlockSpec((None, ng, tc, 4, CHUNK), lambda b, h, j: (b, h, j, 0, 0))],
        out_specs=_prep_out_specs(tc, ng, need_out, seq_of) * 2,
        out_shape=_prep_out(bsz * hg, t_len, need_out) * 2,
        compiler_params=_cp("parallel", "parallel", "parallel"), name="gprep")(qkv, qkv, qkv, rows)


def _g_readout_kernel(of_ref, or_ref, z_ref, nw_ref, o_ref):
    for g in range(of_ref.shape[0]):
        cs = slice(g * LANES, (g + 1) * LANES)
        o = of_ref[g].astype(F32) + or_ref[g].astype(F32)
        on = o * lax.rsqrt(jnp.mean(o * o, axis=-1, keepdims=True) + NORM_EPS) * nw_ref[...]
        o_ref[:, cs] = (on * _silu(z_ref[:, cs].astype(F32))).astype(o_ref.dtype)


def _g_readout(o_f, o_r, p, col0, nw, bsz):
    nseq, t_len, _ = o_f.shape
    hg = nseq // bsz
    ng = _pick(hg, READOUT_GROUP, 1)
    npg = hg // ng
    tt = _pick(t_len, 512, 8)
    nt = t_len // tt
    assert col0 % (ng * LANES) == 0
    c0 = col0 // (ng * LANES)
    seq = pl.BlockSpec((ng, tt, LANES), lambda b, h, i: (b * npg + h, i, 0))
    ztok = pl.BlockSpec((tt, ng * LANES), lambda b, h, i: (b * nt + i, c0 + h))
    tok = pl.BlockSpec((tt, ng * LANES), lambda b, h, i: (b * nt + i, h))
    return pl.pallas_call(
        _g_readout_kernel, grid=(bsz, npg, nt),
        in_specs=[seq, seq, ztok, pl.BlockSpec((1, LANES), lambda b, h, i: (0, 0))],
        out_specs=tok, out_shape=jax.ShapeDtypeStruct((bsz * t_len, hg * LANES), BF16),
        compiler_params=_cp("parallel", "parallel", "parallel"), name="g_readout")(o_f, o_r, p, nw)


def _merge_kernel(oa_ref, ob_ref, pa_ref, pb_ref, ga_ref, gb_ref, o_ref):
    a = _dg(oa_ref[...], pa_ref[...], _NN)
    b = _dg(ob_ref[...], pb_ref[...], _NN)
    ga = _sigmoid(ga_ref[...].astype(F32))
    gb = _sigmoid(gb_ref[...].astype(F32))
    o_ref[...] = (ga * a + gb * b).astype(o_ref.dtype)


def _merge(oa, ob, pa, pb, p, col0, tn):
    m, wa = oa.shape
    wb = ob.shape[1]
    d = pa.shape[1]
    tm = _pick(m, 1024, 8)
    assert d % tn == 0 and col0 % tn == 0
    nj = d // tn
    c0 = col0 // tn
    return pl.pallas_call(
        _merge_kernel, grid=(m // tm, nj),
        in_specs=[pl.BlockSpec((tm, wa), lambda i, j: (i, 0)),
                  pl.BlockSpec((tm, wb), lambda i, j: (i, 0)),
                  pl.BlockSpec((wa, tn), lambda i, j: (0, j)),
                  pl.BlockSpec((wb, tn), lambda i, j: (0, j)),
                  pl.BlockSpec((tm, tn), lambda i, j: (i, c0 + j)),
                  pl.BlockSpec((tm, tn), lambda i, j: (i, c0 + nj + j))],
        out_specs=pl.BlockSpec((tm, tn), lambda i, j: (i, j)),
        out_shape=jax.ShapeDtypeStruct((m, d), BF16),
        compiler_params=_cp("parallel", "parallel"), name="merge")(oa, ob, pa, pb, p, p)


def _outproj_kernel(m_ref, w_ref, x_ref, g_ref, o_ref):
    o_ref[...] = x_ref[...] + g_ref[...] * _dg(m_ref[...], w_ref[...], _NN)


def _outproj(mm, w, x2, gate, rows_per_mod):
    m, k = mm.shape
    d = w.shape[1]
    tm = _pick(rows_per_mod, 1024, 8)
    per = rows_per_mod // tm
    tn = _pick(d, 512, LANES)
    return pl.pallas_call(
        _outproj_kernel, grid=(m // tm, d // tn),
        in_specs=[pl.BlockSpec((tm, k), lambda i, j: (i, 0)),
                  pl.BlockSpec((k, tn), lambda i, j: (0, j)),
                  pl.BlockSpec((tm, tn), lambda i, j: (i, j)),
                  pl.BlockSpec((None, 1, tn), lambda i, j: (i // per, 0, j))],
        out_specs=pl.BlockSpec((tm, tn), lambda i, j: (i, j)),
        out_shape=jax.ShapeDtypeStruct((m, d), F32),
        compiler_params=_cp("parallel", "parallel"), name="outproj")(mm, w, x2, gate)


def _ffn_up_kernel(x_ref, nw_ref, sh_ref, sc_ref, wg_ref, wu_ref, o_ref, h_scr):
    @pl.when(pl.program_id(1) == 0)
    def _():
        h_scr[...] = _normmod_rows(x_ref[...], nw_ref[...], sh_ref[...], sc_ref[...])

    h = h_scr[...]
    g = _dg(h, wg_ref[...], _NN)
    u = _dg(h, wu_ref[...], _NN)
    o_ref[...] = (_silu(g) * u).astype(o_ref.dtype)


def _ffn_up(x2, nw, sh, sc, wgu, rows_per_mod):
    m, d = x2.shape
    fh = wgu.shape[1] // 2
    tm = _pick(rows_per_mod, 1024, 8)
    per = rows_per_mod // tm
    tn = _pick(fh, 512, LANES)
    nj = fh // tn
    return pl.pallas_call(
        _ffn_up_kernel, grid=(m // tm, nj),
        in_specs=[pl.BlockSpec((tm, d), lambda i, j: (i, 0)),
                  pl.BlockSpec((1, d), lambda i, j: (0, 0)),
                  pl.BlockSpec((None, 1, d), lambda i, j: (i // per, 0, 0)),
                  pl.BlockSpec((None, 1, d), lambda i, j: (i // per, 0, 0)),
                  pl.BlockSpec((d, tn), lambda i, j: (0, j)),
                  pl.BlockSpec((d, tn), lambda i, j: (0, j + nj))],
        out_specs=pl.BlockSpec((tm, tn), lambda i, j: (i, j)),
        out_shape=jax.ShapeDtypeStruct((m, fh), BF16),
        scratch_shapes=[pltpu.VMEM((tm, d), BF16)],
        compiler_params=_cp("parallel", "arbitrary"), name="ffn_up")(x2, nw, sh, sc, wgu, wgu)


def _ffn_down_kernel(a_ref, w_ref, x_ref, g_ref, fw_ref, o_ref):
    kk = pl.program_id(1)

    @pl.when(kk == 0)
    def _():
        o_ref[...] = jnp.zeros_like(o_ref)

    o_ref[...] += _dg(a_ref[...], w_ref[...], _NN)

    @pl.when(kk == pl.num_programs(1) - 1)
    def _():
        x = x_ref[...] + g_ref[...] * o_ref[...]
        y = x * lax.rsqrt(jnp.mean(x * x, axis=-1, keepdims=True) + NORM_EPS)
        o_ref[...] = y * fw_ref[...]


def _ffn_down(act, w, x2, gate, fw, rows_per_mod):
    m, fh = act.shape
    d = w.shape[1]
    tm = _pick(rows_per_mod, 1024, 8)
    per = rows_per_mod // tm
    tk = _pick(fh, 512, LANES)
    return pl.pallas_call(
        _ffn_down_kernel, grid=(m // tm, fh // tk),
        in_specs=[pl.BlockSpec((tm, tk), lambda i, k: (i, k)),
                  pl.BlockSpec((tk, d), lambda i, k: (k, 0)),
                  pl.BlockSpec((tm, d), lambda i, k: (i, 0)),
                  pl.BlockSpec((None, 1, d), lambda i, k: (i // per, 0, 0)),
                  pl.BlockSpec((1, d), lambda i, k: (0, 0))],
        out_specs=pl.BlockSpec((tm, d), lambda i, k: (i, 0)),
        out_shape=jax.ShapeDtypeStruct((m, d), F32),
        compiler_params=_cp("parallel", "arbitrary", vmem=VMEM_LIMIT_BIG), name="ffn_down")(act, w, x2, gate, fw)


def _pad_cols(a, n):
    return jnp.pad(a, ((0, 0), (0, n - a.shape[1])))


def _roundup(n, m):
    return -(-n // m) * m


def kernel(x, c, ctx, c_ctx, w_ada, b_ada, norm1_w, w_in, rw_mu, rw_w0, rw_w2, rw_a0, rw_a2, rw_g2, rw_k_k, rw_k_a, rw_r_k, rw_ln_w, rw_ln_b, gdn_conv_w, gdn_a_log, gdn_dt_bias, gdn_norm_w, merge_p_a, merge_p_b, w_out, norm2_w, ffn_w_gate_up, ffn_w_down, final_norm_w):
    assert w_ada.shape[0] == 1, "single-layer trunk"
    bsz, t_len, d = x.shape
    tc_len = ctx.shape[1]
    h_rw, n_rw = rw_r_k.shape[1:]
    wr = h_rw * n_rw
    rk_w, rk_a, rk_g = rw_w2.shape[2], rw_a2.shape[2], rw_g2.shape[1]
    hg, dg = gdn_a_log.shape[-1], gdn_norm_w.shape[-1]
    wg = hg * dg
    assert n_rw == HALF and dg == LANES and h_rw % 2 == 0
    assert t_len % GRID_W == 0 and t_len % CHUNK == 0 and tc_len % CHUNK == 0

    w_in0 = w_in[0]
    o_wd = 3 * wr
    o_ad = o_wd + 2 * rk_w
    o_gd = o_ad + 2 * rk_a
    rw_cols = o_gd + rk_g
    wdp, adp = _roundup(2 * rk_w, LANES), _roundup(2 * rk_a, LANES)
    unit = 512 if all(n % 512 == 0 for n in (wg, d)) else 2 * LANES
    cpw = _roundup(o_wd + wdp + adp + rk_g, unit)
    gdp = cpw - (o_wd + wdp + adp)

    def rw_layout(a):
        return jnp.concatenate([a[:, :o_wd], _pad_cols(a[:, o_wd:o_ad], wdp), _pad_cols(a[:, o_ad:o_gd], adp),
                                _pad_cols(a[:, o_gd:rw_cols], gdp)], axis=1)

    o_cv = rw_cols
    o_z = o_cv + 3 * wg
    o_ab = o_z + wg
    o_gt = o_ab + 4 * hg
    c_cv = cpw
    c_ab = c_cv + 3 * wg
    c_z = c_ab + unit
    c_gt = c_z + wg
    w_all = jnp.concatenate([rw_layout(w_in0[:, :rw_cols]), w_in0[:, o_cv:o_z], _pad_cols(w_in0[:, o_ab:o_gt], unit),
                             w_in0[:, o_z:o_ab], w_in0[:, o_gt:]], axis=1).astype(BF16)
    mu_p = rw_layout(rw_mu)

    w2f = jnp.zeros((wdp, 2 * wr), F32)
    a2f = jnp.zeros((adp, 2 * wr), F32)
    for dd in range(2):
        w2f = w2f.at[dd * rk_w:(dd + 1) * rk_w, dd * wr:(dd + 1) * wr].set(rw_w2[0, dd])
        a2f = a2f.at[dd * rk_a:(dd + 1) * rk_a, dd * wr:(dd + 1) * wr].set(rw_a2[0, dd])
    gd_w = _roundup(rk_g, LANES)
    c_need = o_wd + wdp + adp + gd_w
    g2p = jnp.pad(rw_g2[0], ((0, gd_w - rk_g), (0, 0)))
    li = jnp.arange(LANES)
    e64 = ((li[:, None] // HALF) == (li[None, :] // HALF)).astype(BF16)
    prm = dict(wr=wr, wdp=wdp, adp=adp, cpw=c_need, mu=mu_p[:, :c_need],
               w0=rw_w0[0].reshape(1, 2 * wr), w2=w2f.astype(BF16),
               a0=rw_a0[0].reshape(1, 2 * wr), a2=a2f.astype(BF16), g2=g2p.astype(BF16),
               k_k=rw_k_k, k_a=rw_k_a, r_k=rw_r_k[0].reshape(1, wr), e64=e64)

    nega = _pad_cols((-jnp.exp(gdn_a_log[0])).reshape(1, 2 * hg), LANES)
    dtb = _pad_cols(gdn_dt_bias[0].reshape(1, 2 * hg), LANES)

    rows = _roundup(bsz + 1, 8)
    cc = jnp.concatenate([c, c_ctx[None, :], jnp.zeros((rows - bsz - 1, d), F32)], axis=0)
    mod = _ada(cc, w_ada[0], b_ada)
    mods = [mod[:bsz, i * d:(i + 1) * d].reshape(bsz, 1, d) for i in range(6)]
    sh1, sc1, gt1, sh2, sc2, gt2 = mods
    csh1 = mod[bsz:bsz + 1, 0:d].reshape(1, 1, d)
    csc1 = mod[bsz:bsz + 1, d:2 * d].reshape(1, 1, d)

    x2 = x.reshape(bsz * t_len, d)
    ctx2 = ctx.reshape(bsz * tc_len, d)
    n_all = w_all.shape[1]
    tn_in = max(t for t in range(unit, INPROJ_TN_MAX + 1, unit) if n_all % t == 0 and c_z % t == 0)
    proj = _inproj(x2, norm1_w, sh1, sc1, w_all, n_all, tn_in, t_len)
    proj_c = _inproj(ctx2, norm1_w, csh1, csc1, w_all, c_z, tn_in, bsz * tc_len)

    def mixer_inputs(p, seq_len, latent):
        feat = _rwfeat(p, prm, bsz, seq_len, latent)
        qkv = _gconv(p, c_cv, 3 * wg, unit, gdn_conv_w[0], bsz, seq_len, hg, dg ** -0.5)
        gb = _gfeat(p, c_ab, nega, dtb, hg)
        nc = seq_len // CHUNK
        grows = gb[:, :4 * hg].reshape(bsz, nc, CHUNK, 4, hg).transpose(0, 4, 1, 3, 2)
        return feat, qkv, grows

    feat_c, qkv_c, grows_c = mixer_inputs(proj_c, tc_len, False)
    feat_l, qkv_l, grows_l = mixer_inputs(proj, t_len, True)

    def run_mixer(prep_c, prep_l, nseq):
        s0 = jnp.zeros((2, nseq, LANES, LANES), F32)
        (s_ctx,) = _scan(prep_c[:3], prep_c[3:], s0, False)
        o_f, o_r, _ = _scan(prep_l[:5], prep_l[5:], s_ctx, True)
        return o_f, o_r

    o_f, o_r = run_mixer(_rwprep(feat_c, bsz, tc_len, False), _rwprep(feat_l, bsz, t_len, True),
                         bsz * (wr // LANES))
    o_a = _rw_readout(o_f, o_r, feat_l[7], feat_l[6], rw_ln_w, rw_ln_b, e64, bsz)

    o_f, o_r = run_mixer(_gprep(qkv_c, grows_c, bsz, tc_len, hg, False),
                         _gprep(qkv_l, grows_l, bsz, t_len, hg, True), bsz * hg)
    o_b = _g_readout(o_f, o_r, proj, c_z, gdn_norm_w, bsz)

    mm = _merge(o_a, o_b, merge_p_a[0].astype(BF16), merge_p_b[0].astype(BF16), proj, c_gt, unit)
    x1 = _outproj(mm, w_out[0].astype(BF16), x2, gt1, t_len)
    act = _ffn_up(x1, norm2_w, sh2, sc2, ffn_w_gate_up[0].astype(BF16), t_len)
    out = _ffn_down(act, ffn_w_down[0].astype(BF16), x1, gt2, final_norm_w.reshape(1, d), t_len)
    return out.reshape(bsz, t_len, d)
```
